```python
import jax, jax.numpy as jnp
from jax import lax
import numpy as np

D_MODEL = 1024
BATCH = 32
SEQ = 2048
DEPTH = 2

EPS = 1e-6
CHUNK = 64
HEAD_DIM = 128
HG_HEADS = D_MODEL // 2 // HEAD_DIM
HG_DK = HEAD_DIM
HG_DV = HEAD_DIM
HG_W = HG_HEADS * HG_DK
RET_HEADS = D_MODEL // 2 // HEAD_DIM
RET_DK = HEAD_DIM
RET_DV = HEAD_DIM
RET_W = RET_HEADS * RET_DK
EVEN_SPLITS = (HG_W, HG_W, HG_HEADS * HG_DV, HG_HEADS * HG_DV,
               RET_W, RET_W, RET_HEADS * RET_DV, RET_HEADS * RET_DV)
EVEN_MIX_W = HG_HEADS * HG_DV + RET_HEADS * RET_DV
GLA_HEADS = 4
GLA_K_W = D_MODEL // 2
GLA_V_W = D_MODEL
GLA_DK = GLA_K_W // GLA_HEADS
GLA_DV = GLA_V_W // GLA_HEADS
GLA_RANK = 16
GLA_GATE_NORM = 16.0
ODD_SPLITS = (GLA_K_W, GLA_K_W, GLA_V_W, GLA_V_W, GLA_RANK)
N_EXPERTS = 16
N_GROUPS = 4
EXPERTS_PER_GROUP = N_EXPERTS // N_GROUPS
TOP_K = 2
D_EXPERT = D_MODEL // 2
ROPE_BASE = 10000.0
N_EVEN = (DEPTH + 1) // 2
N_ODD = DEPTH // 2

kernel_name = "hybrid_hgrn2_retnet_gla_grouped_moe_adaln"


def split_cols(a, sizes):
    offs = np.cumsum(sizes)[:-1]
    return jnp.split(a, [int(o) for o in offs], axis=-1)


def rmsnorm(x, w):
    xf = x.astype(jnp.float32)
    y = xf * lax.rsqrt(jnp.mean(xf * xf, axis=-1, keepdims=True) + EPS) * w.astype(jnp.float32)
    return y.astype(x.dtype)


def head_rmsnorm(o, w, n_heads):
    B_, S, W = o.shape
    oh = o.reshape(B_, S, n_heads, W // n_heads)
    oh = oh * lax.rsqrt(jnp.mean(oh * oh, axis=-1, keepdims=True) + EPS)
    return oh.reshape(B_, S, W) * w.astype(jnp.float32)


def to_heads(a, n_heads):
    B_, S, W = a.shape
    return a.reshape(B_, S, n_heads, W // n_heads).transpose(0, 2, 1, 3)


def from_heads(o):
    B_, H, S, d = o.shape
    return o.transpose(0, 2, 1, 3).reshape(B_, S, H * d)


def rotary(x):
    S, dh = x.shape[2], x.shape[3]
    inv = ROPE_BASE ** (-jnp.arange(0, dh, 2, dtype=jnp.float32) / dh)
    ang = jnp.arange(S, dtype=jnp.float32)[:, None] * inv[None, :]
    cos, sin = jnp.cos(ang), jnp.sin(ang)
    x1, x2 = x[..., : dh // 2], x[..., dh // 2:]
    return jnp.concatenate([x1 * cos - x2 * sin, x2 * cos + x1 * sin], axis=-1)


def chunk_gated_linear_attention(q, k, v, log_g):
    f32 = jnp.float32
    B_, H, S, dk = q.shape
    dv = v.shape[-1]
    n = S // CHUNK
    q = q.astype(f32).reshape(B_, H, n, CHUNK, dk)
    k = k.astype(f32).reshape(B_, H, n, CHUNK, dk)
    v = v.astype(f32).reshape(B_, H, n, CHUNK, dv)
    lg = log_g.astype(f32)
    lg = lg.reshape(lg.shape[0], H, n, CHUNK, lg.shape[-1])
    b = jnp.cumsum(lg, axis=3)
    b_last = b[:, :, :, CHUNK - 1:, :]
    b_mid = b[:, :, :, CHUNK // 2 - 1: CHUNK // 2, :]
    causal = jnp.tril(jnp.ones((CHUNK, CHUNK), dtype=bool))
    scores = jnp.einsum('bhntk,bhnsk->bhnts', q * jnp.exp(b - b_mid), k * jnp.exp(b_mid - b))
    o_intra = jnp.einsum('bhnts,bhnsv->bhntv', jnp.where(causal, scores, 0.0), v)
    q_inter = q * jnp.exp(b)
    u = jnp.einsum('bhnsk,bhnsv->bhnkv', k * jnp.exp(b_last - b), v)
    decay = jnp.exp(b_last[:, :, :, 0, :])

    def step(state, inp):
        q_c, u_c, d_c = inp
        o_c = jnp.einsum('bhtk,bhkv->bhtv', q_c, state)
        return d_c[..., None] * state + u_c, o_c

    state0 = jnp.zeros((B_, H, dk, dv), f32)
    _, o_inter = lax.scan(step, state0, (jnp.moveaxis(q_inter, 2, 0),
                                          jnp.moveaxis(u, 2, 0),
                                          jnp.moveaxis(decay, 2, 0)))
    o = o_intra + jnp.moveaxis(o_inter, 0, 2)
    return o.reshape(B_, H, S, dv)


def even_mixer(h, w_in, lb, hg_norm_w, ret_norm_w, w_out):
    f32 = jnp.float32
    S = h.shape[1]
    proj = h @ w_in
    hq, hf, hi, hg, rq, rk, rv, rg = split_cols(proj, EVEN_SPLITS)
    f = lb + (1.0 - lb) * jax.nn.sigmoid(hf.astype(f32))
    o_hg = chunk_gated_linear_attention(
        to_heads(jax.nn.sigmoid(hq.astype(f32)), HG_HEADS),
        to_heads(1.0 - f, HG_HEADS),
        to_heads(hi, HG_HEADS),
        to_heads(jnp.log(f), HG_HEADS))
    o_hg = head_rmsnorm(from_heads(o_hg), hg_norm_w, HG_HEADS) * jax.nn.silu(hg.astype(f32))
    log_gamma = jnp.log1p(-jnp.exp2(-5.0 - jnp.arange(RET_HEADS, dtype=f32)))
    log_g = jnp.broadcast_to(log_gamma[None, :, None, None], (1, RET_HEADS, S, 1))
    q_r = rotary(to_heads(rq.astype(f32), RET_HEADS))
    k_r = rotary(to_heads(rk.astype(f32), RET_HEADS)) * (RET_DK ** -0.5)
    o_ret = chunk_gated_linear_attention(q_r, k_r, to_heads(rv, RET_HEADS), log_g)
    o_ret = head_rmsnorm(from_heads(o_ret), ret_norm_w, RET_HEADS) * jax.nn.silu(rg.astype(f32))
    o = jnp.concatenate([o_hg, o_ret], axis=-1).astype(h.dtype)
    return o @ w_out


def odd_mixer(h, w_in, w_a2, b_a2, gla_norm_w, w_out):
    f32 = jnp.float32
    proj = h @ w_in
    gq, gk, gv, gg, ga = split_cols(proj, ODD_SPLITS)
    log_a = jax.nn.log_sigmoid((ga @ w_a2 + b_a2).astype(f32)) / GLA_GATE_NORM
    o = chunk_gated_linear_attention(
        to_heads(gq.astype(f32) * (GLA_DK ** -0.5), GLA_HEADS),
        to_heads(gk, GLA_HEADS),
        to_heads(gv, GLA_HEADS),
        to_heads(log_a, GLA_HEADS))
    o = head_rmsnorm(from_heads(o), gla_norm_w, GLA_HEADS) * jax.nn.silu(gg.astype(f32))
    return o.astype(h.dtype) @ w_out


def grouped_moe(h, w_router, b_router, w_gate, w_up, w_down):
    f32 = jnp.float32
    B_, S, D = h.shape
    t = h.reshape(-1, D)
    probs = jax.nn.softmax((t @ w_router).astype(f32), axis=-1)
    sel = probs + b_router.astype(f32)
    group_score = sel.reshape(-1, N_GROUPS, EXPERTS_PER_GROUP).max(axis=-1)
    group_mask = jax.nn.one_hot(jnp.argmax(group_score, axis=-1), N_GROUPS, dtype=bool)
    expert_mask = jnp.repeat(group_mask, EXPERTS_PER_GROUP, axis=-1)
    _, top_i = lax.top_k(jnp.where(expert_mask, sel, -jnp.inf), TOP_K)
    top_w = jnp.take_along_axis(probs, top_i, axis=-1)
    top_w = top_w / jnp.sum(top_w, axis=-1, keepdims=True)
    combine = jnp.sum(jax.nn.one_hot(top_i, N_EXPERTS, dtype=f32) * top_w[..., None], axis=1)

    def expert(acc, inp):
        wg, wu, wd, cw = inp
        y = (jax.nn.silu(t @ wg) * (t @ wu)) @ wd
        return acc + cw[:, None] * y.astype(f32), None

    out, _ = lax.scan(expert, jnp.zeros(t.shape, f32), (w_gate, w_up, w_down, combine.T))
    return out.astype(h.dtype).reshape(B_, S, D)


def setup_inputs(seed: int = 0) -> dict:
    key = jax.random.key(seed)
    ks = jax.random.split(key, 24)
    nrm = jax.random.normal
    D = D_MODEL
    return {
        "x": nrm(ks[0], (BATCH, SEQ, D), jnp.float32),
        "c": nrm(ks[1], (BATCH, D), jnp.float32),
        "w_ada": nrm(ks[2], (DEPTH, D, 6 * D), jnp.float32) * (0.5 * D ** -0.5),
        "b_ada": nrm(ks[3], (DEPTH, 6 * D), jnp.float32) * 0.02,
        "norm_mix_w": 1.0 + 0.05 * nrm(ks[4], (DEPTH, D), jnp.float32),
        "norm_ffn_w": 1.0 + 0.05 * nrm(ks[5], (DEPTH, D), jnp.float32),
        "w_in_even": nrm(ks[6], (N_EVEN, D, sum(EVEN_SPLITS)), jnp.float32) * D ** -0.5,
        "hg_lb_logits": 0.1 * nrm(ks[7], (DEPTH + 1, HG_W), jnp.float32),
        "hg_norm_w": 1.0 + 0.05 * nrm(ks[8], (N_EVEN, HG_HEADS * HG_DV), jnp.float32),
        "ret_norm_w": 1.0 + 0.05 * nrm(ks[9], (N_EVEN, RET_HEADS * RET_DV), jnp.float32),
        "w_out_even": nrm(ks[10], (N_EVEN, EVEN_MIX_W, D), jnp.float32) * EVEN_MIX_W ** -0.5,
        "w_in_odd": nrm(ks[11], (N_ODD, D, sum(ODD_SPLITS)), jnp.float32) * D ** -0.5,
        "w_gla_a2": nrm(ks[12], (N_ODD, GLA_RANK, GLA_K_W), jnp.float32) * GLA_RANK ** -0.5,
        "b_gla_a2": 0.02 * nrm(ks[13], (N_ODD, GLA_K_W), jnp.float32),
        "gla_norm_w": 1.0 + 0.05 * nrm(ks[14], (N_ODD, GLA_V_W), jnp.float32),
        "w_out_odd": nrm(ks[15], (N_ODD, GLA_V_W, D), jnp.float32) * GLA_V_W ** -0.5,
        "w_router": nrm(ks[16], (D, N_EXPERTS), jnp.float32) * D ** -0.5,
        "b_router": 0.01 * nrm(ks[17], (N_EXPERTS,), jnp.float32),
        "w_gate": nrm(ks[18], (DEPTH, N_EXPERTS, D, D_EXPERT), jnp.float32) * D ** -0.5,
        "w_up": nrm(ks[19], (DEPTH, N_EXPERTS, D, D_EXPERT), jnp.float32) * D ** -0.5,
        "w_down": nrm(ks[20], (DEPTH, N_EXPERTS, D_EXPERT, D), jnp.float32) * D_EXPERT ** -0.5,
        "final_norm_w": 1.0 + 0.05 * nrm(ks[21], (D,), jnp.float32),
    }


def reference(x, c, w_ada, b_ada, norm_mix_w, norm_ffn_w, w_in_even, hg_lb_logits,
              hg_norm_w, ret_norm_w, w_out_even, w_in_odd, w_gla_a2, b_gla_a2,
              gla_norm_w, w_out_odd, w_router, b_router, w_gate, w_up, w_down,
              final_norm_w):
    lb_table = jnp.cumsum(jax.nn.softmax(hg_lb_logits.astype(jnp.float32), axis=0), axis=0)
    c_act = jax.nn.silu(c)
    for l in range(DEPTH):
        mod = c_act @ w_ada[l] + b_ada[l]
        sh1, sc1, g1, sh2, sc2, g2 = jnp.split(mod[:, None, :], 6, axis=-1)
        h = rmsnorm(x, norm_mix_w[l]) * (1.0 + sc1) + sh1
        j = l // 2
        if l % 2 == 0:
            y = even_mixer(h, w_in_even[j], lb_table[l], hg_norm_w[j], ret_norm_w[j], w_out_even[j])
        else:
            y = odd_mixer(h, w_in_odd[j], w_gla_a2[j], b_gla_a2[j], gla_norm_w[j], w_out_odd[j])
        x = x + g1 * y
        h = rmsnorm(x, norm_ffn_w[l]) * (1.0 + sc2) + sh2
        x = x + g2 * grouped_moe(h, w_router, b_router, w_gate[l], w_up[l], w_down[l])
    return rmsnorm(x, final_norm_w)
```

```python
import functools

import jax
import jax.numpy as jnp
import numpy as np
from jax import lax
from jax.experimental import pallas as pl
from jax.experimental.pallas import tpu as pltpu

F32 = jnp.float32
BF16 = jnp.bfloat16
HIGHEST = lax.Precision.HIGHEST

EPS = 1e-6
CHUNK = 64
HEAD_DIM = 128
N_HEADS = 4
N_EXPERTS = 16
N_GROUPS = 4
EXPERTS_PER_GROUP = N_EXPERTS // N_GROUPS
GLA_RANK = 16
GLA_GATE_NORM = 16.0
ROPE_BASE = 10000.0
LANES = 128
VMEM_LIMIT = 52 * 1024 * 1024


def _cparams(sem):
    return pltpu.CompilerParams(dimension_semantics=sem, vmem_limit_bytes=VMEM_LIMIT)


def _sigmoid(z):
    return 1.0 / (1.0 + jnp.exp(-z))


def _silu(z):
    return z * _sigmoid(z)


def _mod_kernel(c_ref, w_ref, b_ref, o_ref):
    c = c_ref[...]
    o_ref[0] = jnp.dot(_silu(c), w_ref[0], preferred_element_type=F32, precision=HIGHEST) + b_ref[0]


def _adaln_mod(c, w_ada, b_ada):
    depth, d, n = w_ada.shape
    bsz = c.shape[0]
    tn = 1536
    return pl.pallas_call(
        _mod_kernel,
        grid=(depth, n // tn),
        in_specs=[pl.BlockSpec((bsz, d), lambda l, j: (0, 0)),
                  pl.BlockSpec((1, d, tn), lambda l, j: (l, 0, j)),
                  pl.BlockSpec((1, 1, tn), lambda l, j: (l, 0, j))],
        out_specs=pl.BlockSpec((1, bsz, tn), lambda l, j: (l, 0, j)),
        out_shape=jax.ShapeDtypeStruct((depth, bsz, n), F32),
        compiler_params=_cparams(("arbitrary", "arbitrary")),
        name="adaln_mod",
    )(c, w_ada, b_ada.reshape(depth, 1, n))


def _norm_mod(x, nw, mod_ref, shift_row, scale_row):
    ms = jnp.mean(x * x, axis=-1, keepdims=True)
    return x * lax.rsqrt(ms + EPS) * (nw * (1.0 + mod_ref[scale_row:scale_row + 1, :])) + mod_ref[shift_row:shift_row + 1, :]


def _inproj_even_kernel(x_ref, mod_ref, nw_ref, w_ref, o_ref, f_ref):
    h = _norm_mod(x_ref[...], nw_ref[...], mod_ref, 0, 1)
    res = jnp.dot(h.astype(BF16), w_ref[...], preferred_element_type=F32)
    o_ref[...] = res.astype(BF16)
    f_ref[...] = res[:, 512:1024]


def _log_sigmoid(z):
    return jnp.minimum(z, 0.0) - jnp.log1p(jnp.exp(-jnp.abs(z)))


def _inproj_odd_kernel(x_ref, mod_ref, nw_ref, w_ref, wa_ref, wa2_ref, ba2_ref, o_ref, la_ref):
    h = _norm_mod(x_ref[...], nw_ref[...], mod_ref, 0, 1).astype(BF16)
    o_ref[...] = jnp.dot(h, w_ref[...], preferred_element_type=F32).astype(BF16)
    ga = jnp.dot(h, wa_ref[...], preferred_element_type=F32)
    z = jnp.dot(ga.astype(BF16), wa2_ref[...], preferred_element_type=F32) + ba2_ref[...]
    la_ref[...] = _log_sigmoid(z) * (1.0 / GLA_GATE_NORM)


def _inproj(x2, mod4, layer, seq, nw, weights, kernel_fn, n_main, extra_specs, extra_args):
    t, d = x2.shape
    tm = min(256, seq)
    per_b = seq // tm
    return pl.pallas_call(
        kernel_fn,
        grid=(t // tm,),
        in_specs=[pl.BlockSpec((tm, d), lambda i: (i, 0)),
                  pl.BlockSpec((None, None, 6, d), lambda i: (layer, i // per_b, 0, 0)),
                  pl.BlockSpec((1, d), lambda i: (0, 0)),
                  pl.BlockSpec((d, n_main), lambda i: (0, 0))] + extra_specs,
        out_specs=[pl.BlockSpec((tm, n_main), lambda i: (i, 0)),
                   pl.BlockSpec((tm, 512), lambda i: (i, 0))],
        out_shape=[jax.ShapeDtypeStruct((t, n_main), BF16),
                   jax.ShapeDtypeStruct((t, 512), F32)],
        compiler_params=_cparams(("parallel",)),
        name=f"inproj_l{layer}",
    )(x2, mod4, nw.reshape(1, d), weights, *extra_args)


def _causal_mask():
    r = lax.broadcasted_iota(jnp.int32, (CHUNK, CHUNK), 0)
    c = lax.broadcasted_iota(jnp.int32, (CHUNK, CHUNK), 1)
    return r >= c


def _dot_nt(a, b):
    return lax.dot_general(a, b, (((1,), (1,)), ((), ())), preferred_element_type=F32)


def _dot_tn(a, b):
    return lax.dot_general(a, b, (((0,), (0,)), ((), ())), preferred_element_type=F32)


def _head_out(o, g, nw, n_norm):
    ms = jnp.sum(o * o, axis=-1, keepdims=True) * (1.0 / n_norm)
    return o * lax.rsqrt(ms + EPS) * nw * _silu(g)


def _decay_chunk(q, k, v, b, st_ref):
    b_mid = b[CHUNK // 2 - 1:CHUNK // 2, :]
    b_last = b[CHUNK - 1:CHUNK, :]
    qs = (q * jnp.exp(b - b_mid)).astype(BF16)
    ks = (k * jnp.exp(b_mid - b)).astype(BF16)
    scores = jnp.where(_causal_mask(), _dot_nt(qs, ks), 0.0)
    o = jnp.dot(scores.astype(BF16), v, preferred_element_type=F32)
    st = st_ref[...]
    o = o + _dot_nt((q * jnp.exp(b)).astype(BF16), st.astype(BF16))
    kd = (k * jnp.exp(b_last - b)).astype(BF16)
    st_ref[...] = jnp.exp(b_last) * st + _dot_tn(v, kd)
    return o


def _cumsum_rows(lg):
    tril = _causal_mask().astype(F32)
    return jnp.dot(tril, lg, preferred_element_type=F32, precision=HIGHEST)


def _hgrn_kernel(q_ref, f_ref, v_ref, g_ref, lbl_ref, nw_ref, o_ref, st_ref, *, layer, n_chunks):
    lbl = lbl_ref[...]
    e = jnp.exp(lbl - jnp.max(lbl, axis=0, keepdims=True))
    p = e / jnp.sum(e, axis=0, keepdims=True)
    lb = jnp.sum(p[:layer + 1], axis=0, keepdims=True)
    nw = nw_ref[...]
    st_ref[...] = jnp.zeros_like(st_ref)

    def body(c, carry):
        sl = pl.ds(pl.multiple_of(c * CHUNK, CHUNK), CHUNK)
        f = lb + (1.0 - lb) * _sigmoid(f_ref[sl, :])
        b = _cumsum_rows(jnp.log(f))
        q = _sigmoid(q_ref[sl, :].astype(F32))
        o = _decay_chunk(q, 1.0 - f, v_ref[sl, :], b, st_ref)
        o_ref[sl, :] = _head_out(o, g_ref[sl, :].astype(F32), nw, HEAD_DIM).astype(BF16)
        return carry

    lax.fori_loop(0, n_chunks, body, 0)


def _gla_kernel(q_ref, k_ref, v_ref, g_ref, la_ref, nw_ref, o_ref, st_ref, *, n_chunks, dv):
    nw = nw_ref[...]
    st_ref[...] = jnp.zeros_like(st_ref)

    def body(c, carry):
        sl = pl.ds(pl.multiple_of(c * CHUNK, CHUNK), CHUNK)
        b = _cumsum_rows(la_ref[sl, :])
        q = q_ref[sl, :].astype(F32) * (HEAD_DIM ** -0.5)
        o = _decay_chunk(q, k_ref[sl, :].astype(F32), v_ref[sl, :], b, st_ref)
        o_ref[sl, :] = _head_out(o, g_ref[sl, :].astype(F32), nw, dv).astype(BF16)
        return carry

    lax.fori_loop(0, n_chunks, body, 0)


def _ret_kernel(q_ref, k_ref, v_ref, g_ref, cos_ref, sin_ref, nw_ref, o_ref, st_ref, *, n_chunks):
    head = pl.program_id(1).astype(F32)
    hv = jnp.full((1, LANES), head, F32)
    lg = jnp.log1p(-jnp.exp2(-5.0 - hv))
    pos = lax.broadcasted_iota(jnp.int32, (CHUNK, LANES), 0).astype(F32)
    e_q = jnp.exp((pos + 1.0) * lg)
    e_k = jnp.exp((CHUNK - 1.0 - pos) * lg)
    e_c = jnp.exp(CHUNK * lg)
    r = lax.broadcasted_iota(jnp.int32, (CHUNK, CHUNK), 0)
    cc = lax.broadcasted_iota(jnp.int32, (CHUNK, CHUNK), 1)
    dmask = jnp.where(r >= cc, jnp.exp((r - cc).astype(F32) * lg[:, :CHUNK]), 0.0)
    nw = nw_ref[...]
    st_ref[...] = jnp.zeros_like(st_ref)

    def rot(xx, cs, sn):
        return xx * cs + pltpu.roll(xx, HEAD_DIM // 2, 1) * sn

    def body(c, carry):
        sl = pl.ds(pl.multiple_of(c * CHUNK, CHUNK), CHUNK)
        cs = cos_ref[sl, :]
        sn = sin_ref[sl, :]
        q = rot(q_ref[sl, :].astype(F32), cs, sn)
        k = rot(k_ref[sl, :].astype(F32), cs, sn) * (HEAD_DIM ** -0.5)
        v = v_ref[sl, :]
        scores = _dot_nt(q.astype(BF16), k.astype(BF16)) * dmask
        o = jnp.dot(scores.astype(BF16), v, preferred_element_type=F32)
        st = st_ref[...]
        o = o + _dot_nt((q * e_q).astype(BF16), st.astype(BF16))
        st_ref[...] = e_c * st + _dot_tn(v, (k * e_k).astype(BF16))
        o_ref[sl, :] = _head_out(o, g_ref[sl, :].astype(F32), nw, HEAD_DIM).astype(BF16)
        return carry

    lax.fori_loop(0, n_chunks, body, 0)


def _col(width, base):
    return lambda seq: pl.BlockSpec((seq, width), lambda b, h: (b, base + h))


def _mixer_call(kernel_fn, bsz, seq, in_specs, out_width, dv, dk, name, args):
    return pl.pallas_call(
        kernel_fn,
        grid=(bsz, N_HEADS),
        in_specs=in_specs,
        out_specs=pl.BlockSpec((seq, out_width), lambda b, h: (b, h)),
        out_shape=jax.ShapeDtypeStruct((bsz * seq, N_HEADS * out_width), BF16),
        scratch_shapes=[pltpu.VMEM((dv, dk), F32)],
        compiler_params=_cparams(("parallel", "parallel")),
        name=name,
    )(*args)


def _even_mixers(proj, hf, lb_logits, hg_nw, ret_nw, bsz, seq, layer):
    n_chunks = seq // CHUNK
    hd = HEAD_DIM
    spec = lambda base: pl.BlockSpec((seq, hd), lambda b, h: (b, base + h))
    vec = lambda rows: pl.BlockSpec((rows, hd), lambda b, h: (0, h))
    o_hg = _mixer_call(
        functools.partial(_hgrn_kernel, layer=layer, n_chunks=n_chunks), bsz, seq,
        [spec(0), pl.BlockSpec((seq, hd), lambda b, h: (b, h)), spec(8), spec(12), vec(lb_logits.shape[0]), vec(1)],
        hd, hd, hd, f"hgrn_l{layer}",
        (proj, hf, proj, proj, lb_logits, hg_nw.reshape(1, -1)))
    inv = ROPE_BASE ** (-jnp.arange(0, hd, 2, dtype=F32) / hd)
    ang = jnp.arange(seq, dtype=F32)[:, None] * inv[None, :]
    cos2 = jnp.concatenate([jnp.cos(ang), jnp.cos(ang)], axis=-1)
    sin2 = jnp.concatenate([-jnp.sin(ang), jnp.sin(ang)], axis=-1)
    tab = pl.BlockSpec((seq, hd), lambda b, h: (0, 0))
    o_ret = _mixer_call(
        functools.partial(_ret_kernel, n_chunks=n_chunks), bsz, seq,
        [spec(16), spec(20), spec(24), spec(28), tab, tab, vec(1)],
        hd, hd, hd, f"ret_l{layer}",
        (proj, proj, proj, proj, cos2, sin2, ret_nw.reshape(1, -1)))
    return o_hg, o_ret


def _odd_mixer(proj, log_a, gla_nw, bsz, seq, layer):
    n_chunks = seq // CHUNK
    hd = HEAD_DIM
    dv = 2 * hd
    spec = lambda base: pl.BlockSpec((seq, hd), lambda b, h: (b, base + h))
    wide = lambda base: pl.BlockSpec((seq, dv), lambda b, h: (b, base + h))
    return _mixer_call(
        functools.partial(_gla_kernel, n_chunks=n_chunks, dv=dv), bsz, seq,
        [spec(0), spec(4), wide(4), wide(8), pl.BlockSpec((seq, hd), lambda b, h: (b, h)),
         pl.BlockSpec((1, dv), lambda b, h: (0, h))],
        dv, dv, hd, f"gla_l{layer}",
        (proj, proj, proj, proj, log_a, gla_nw.reshape(1, -1)))


def _route(h2, wr, br):
    neg = -jnp.inf
    logits = jnp.dot(h2, wr, preferred_element_type=F32, precision=HIGHEST)
    lane = lax.broadcasted_iota(jnp.int32, logits.shape, 1)
    valid = lane < N_EXPERTS
    lm = jnp.where(valid, logits, neg)
    ex = jnp.where(valid, jnp.exp(lm - jnp.max(lm, axis=-1, keepdims=True)), 0.0)
    probs = ex / jnp.sum(ex, axis=-1, keepdims=True)
    sel = jnp.where(valid, probs + br, neg)
    grp = lane // EXPERTS_PER_GROUP
    gs = [jnp.max(jnp.where(grp == g, sel, neg), axis=-1, keepdims=True) for g in range(N_GROUPS)]
    best = functools.reduce(jnp.maximum, gs)
    gidx = jnp.full(best.shape, N_GROUPS - 1, jnp.int32)
    for g in range(N_GROUPS - 2, -1, -1):
        gidx = jnp.where(gs[g] == best, g, gidx)
    masked = jnp.where(grp == gidx, sel, neg)
    m1 = jnp.max(masked, axis=-1, keepdims=True)
    i1 = jnp.min(jnp.where(masked == m1, lane, LANES), axis=-1, keepdims=True)
    masked2 = jnp.where(lane == i1, neg, masked)
    m2 = jnp.max(masked2, axis=-1, keepdims=True)
    i2 = jnp.min(jnp.where(masked2 == m2, lane, LANES), axis=-1, keepdims=True)
    p1 = jnp.sum(jnp.where(lane == i1, probs, 0.0), axis=-1, keepdims=True)
    p2 = jnp.sum(jnp.where(lane == i2, probs, 0.0), axis=-1, keepdims=True)
    den = p1 + p2
    return jnp.where(lane == i1, p1 / den, 0.0) + jnp.where(lane == i2, p2 / den, 0.0)


def _outproj_kernel(*refs, n_o):
    o_refs = refs[:n_o]
    w_refs = refs[n_o:2 * n_o]
    x_ref, mod_ref, nw_ref, wr_ref, br_ref, x1_ref, h2_ref, cw_ref = refs[2 * n_o:]
    y = jnp.dot(o_refs[0][...], w_refs[0][...], preferred_element_type=F32)
    for o_ref, w_ref in zip(o_refs[1:], w_refs[1:]):
        y = y + jnp.dot(o_ref[...], w_ref[...], preferred_element_type=F32)
    x1 = x_ref[...] + mod_ref[2:3, :] * y
    x1_ref[...] = x1
    h2 = _norm_mod(x1, nw_ref[...], mod_ref, 3, 4)
    h2_ref[...] = h2.astype(BF16)
    cw_ref[...] = _route(h2, wr_ref[...], br_ref[...])


def _outproj_route(o_list, w_list, x2, mod4, layer, seq, ffn_nw, wr_pad, br_pad):
    t, d = x2.shape
    tm = min(256, seq)
    per_b = seq // tm
    n_o = len(o_list)
    full = lambda a: pl.BlockSpec(a.shape, lambda i: (0,) * a.ndim)
    in_specs = ([pl.BlockSpec((tm, o.shape[1]), lambda i: (i, 0)) for o in o_list]
                + [full(w) for w in w_list]
                + [pl.BlockSpec((tm, d), lambda i: (i, 0)),
                   pl.BlockSpec((None, None, 6, d), lambda i: (layer, i // per_b, 0, 0)),
                   pl.BlockSpec((1, d), lambda i: (0, 0)),
                   full(wr_pad), full(br_pad)])
    return pl.pallas_call(
        functools.partial(_outproj_kernel, n_o=n_o),
        grid=(t // tm,),
        in_specs=in_specs,
        out_specs=[pl.BlockSpec((tm, d), lambda i: (i, 0)),
                   pl.BlockSpec((tm, d), lambda i: (i, 0)),
                   pl.BlockSpec((tm, LANES), lambda i: (i, 0))],
        out_shape=[jax.ShapeDtypeStruct((t, d), F32),
                   jax.ShapeDtypeStruct((t, d), BF16),
                   jax.ShapeDtypeStruct((t, LANES), F32)],
        compiler_params=_cparams(("parallel",)),
        name=f"outproj_route_l{layer}",
    )(*o_list, *w_list, x2, mod4, ffn_nw.reshape(1, d), wr_pad, br_pad)


def _moe_kernel(h_ref, cw_ref, wg_ref, wu_ref, wd_ref, x_ref, mod_ref, fw_ref, o_ref, acc_ref, *, final):
    e = pl.program_id(1)

    @pl.when(e == 0)
    def _():
        acc_ref[...] = jnp.zeros_like(acc_ref)

    h = h_ref[...]
    g = jnp.dot(h, wg_ref[0], preferred_element_type=F32)
    u = jnp.dot(h, wu_ref[0], preferred_element_type=F32)
    y = jnp.dot((_silu(g) * u).astype(BF16), wd_ref[0], preferred_element_type=F32)
    cw = cw_ref[...]
    lane = lax.broadcasted_iota(jnp.int32, cw.shape, 1)
    acc_ref[...] += jnp.sum(jnp.where(lane == e, cw, 0.0), axis=-1, keepdims=True) * y

    @pl.when(e == N_EXPERTS - 1)
    def _():
        xn = x_ref[...] + mod_ref[5:6, :] * acc_ref[...]
        if final:
            ms = jnp.mean(xn * xn, axis=-1, keepdims=True)
            xn = xn * lax.rsqrt(ms + EPS) * fw_ref[...]
        o_ref[...] = xn


def _moe(h2, cw, wg, wu, wd, x1, mod4, layer, seq, final_w, final):
    t, d = x1.shape
    de = wg.shape[-1]
    tm = min(1024, seq)
    per_b = seq // tm
    return pl.pallas_call(
        functools.partial(_moe_kernel, final=final),
        grid=(t // tm, N_EXPERTS),
        in_specs=[pl.BlockSpec((tm, d), lambda i, e: (i, 0)),
                  pl.BlockSpec((tm, LANES), lambda i, e: (i, 0)),
                  pl.BlockSpec((1, d, de), lambda i, e: (e, 0, 0)),
                  pl.BlockSpec((1, d, de), lambda i, e: (e, 0, 0)),
                  pl.BlockSpec((1, de, d), lambda i, e: (e, 0, 0)),
                  pl.BlockSpec((tm, d), lambda i, e: (i, 0)),
                  pl.BlockSpec((None, None, 6, d), lambda i, e: (layer, i // per_b, 0, 0)),
                  pl.BlockSpec((1, d), lambda i, e: (0, 0))],
        out_specs=pl.BlockSpec((tm, d), lambda i, e: (i, 0)),
        out_shape=jax.ShapeDtypeStruct((t, d), F32),
        scratch_shapes=[pltpu.VMEM((tm, d), F32)],
        compiler_params=_cparams(("parallel", "arbitrary")),
        name=f"moe_l{layer}",
    )(h2, cw, wg, wu, wd, x1, mod4, final_w.reshape(1, d))


def kernel(x, c, w_ada, b_ada, norm_mix_w, norm_ffn_w, w_in_even, hg_lb_logits, hg_norm_w, ret_norm_w,
           w_out_even, w_in_odd, w_gla_a2, b_gla_a2, gla_norm_w, w_out_odd, w_router, b_router,
           w_gate, w_up, w_down, final_norm_w):
    bsz, seq, d = x.shape
    depth = w_ada.shape[0]
    x2 = x.reshape(bsz * seq, d)
    mod4 = _adaln_mod(c, w_ada, b_ada).reshape(depth, bsz, 6, d)
    wr_pad = jnp.pad(w_router, ((0, 0), (0, LANES - N_EXPERTS)))
    br_pad = jnp.pad(b_router, (0, LANES - N_EXPERTS)).reshape(1, LANES)
    for l in range(depth):
        j = l // 2
        if l % 2 == 0:
            proj, hf = _inproj(x2, mod4, l, seq, norm_mix_w[l], w_in_even[j].astype(BF16),
                               _inproj_even_kernel, w_in_even.shape[-1], [], [])
            o_hg, o_ret = _even_mixers(proj, hf, hg_lb_logits, hg_norm_w[j], ret_norm_w[j], bsz, seq, l)
            wo = w_out_even[j].astype(BF16)
            half = o_hg.shape[1]
            o_list, w_list = [o_hg, o_ret], [wo[:half], wo[half:]]
        else:
            n_main = w_in_odd.shape[-1] - GLA_RANK
            wi = w_in_odd[j]
            wa = jnp.pad(wi[:, n_main:], ((0, 0), (0, LANES - GLA_RANK))).astype(BF16)
            wa2 = jnp.pad(w_gla_a2[j], ((0, LANES - GLA_RANK), (0, 0))).astype(BF16)
            full = lambda a: pl.BlockSpec(a.shape, lambda i: (0,) * a.ndim)
            ba2 = b_gla_a2[j].reshape(1, -1)
            proj, log_a = _inproj(x2, mod4, l, seq, norm_mix_w[l], wi[:, :n_main].astype(BF16),
                                  _inproj_odd_kernel, n_main, [full(wa), full(wa2), full(ba2)], [wa, wa2, ba2])
            o_list, w_list = [_odd_mixer(proj, log_a, gla_norm_w[j], bsz, seq, l)], [w_out_odd[j].astype(BF16)]
        x1, h2, cw = _outproj_route(o_list, w_list, x2, mod4, l, seq, norm_ffn_w[l], wr_pad, br_pad)
        x2 = _moe(h2, cw, w_gate[l].astype(BF16), w_up[l].astype(BF16), w_down[l].astype(BF16),
                  x1, mod4, l, seq, final_norm_w, final=(l == depth - 1))
    return x2.reshape(bsz, seq, d)
```

```python
import functools

import jax
import jax.numpy as jnp
from jax import lax
from jax.experimental import pallas as pl
from jax.experimental.pallas import tpu as pltpu

F32 = jnp.float32
BF16 = jnp.bfloat16
HIGHEST = lax.Precision.HIGHEST

EPS = 1e-6
CHUNK = 64
HEAD_DIM = 128
N_HEADS = 4
N_EXPERTS = 16
N_GROUPS = 4
EXPERTS_PER_GROUP = N_EXPERTS // N_GROUPS
GLA_RANK = 16
GLA_GATE_NORM = 16.0
ROPE_BASE = 10000.0
LANES = 128
VMEM_LIMIT = 52 * 1024 * 1024
MIX_SEQ_TILE = 512


def _cparams(sem):
    return pltpu.CompilerParams(dimension_semantics=sem, vmem_limit_bytes=VMEM_LIMIT)


def _sigmoid(z):
    return 0.5 * jnp.tanh(0.5 * z) + 0.5


def _silu(z):
    return z * _sigmoid(z)


def _mod_kernel(c_ref, w_ref, b_ref, o_ref):
    c = c_ref[...]
    o_ref[0] = jnp.dot(_silu(c), w_ref[0], preferred_element_type=F32, precision=HIGHEST) + b_ref[0]


def _adaln_mod(c, w_ada, b_ada):
    depth, d, n = w_ada.shape
    bsz = c.shape[0]
    tn = 1536
    return pl.pallas_call(
        _mod_kernel,
        grid=(depth, n // tn),
        in_specs=[pl.BlockSpec((bsz, d), lambda l, j: (0, 0)),
                  pl.BlockSpec((1, d, tn), lambda l, j: (l, 0, j)),
                  pl.BlockSpec((1, 1, tn), lambda l, j: (l, 0, j))],
        out_specs=pl.BlockSpec((1, bsz, tn), lambda l, j: (l, 0, j)),
        out_shape=jax.ShapeDtypeStruct((depth, bsz, n), F32),
        compiler_params=_cparams(("arbitrary", "arbitrary")),
        name="adaln_mod",
    )(c, w_ada, b_ada.reshape(depth, 1, n))


def _norm_mod(x, nw, mod_ref, shift_row, scale_row):
    ms = jnp.mean(x * x, axis=-1, keepdims=True)
    return x * lax.rsqrt(ms + EPS) * (nw * (1.0 + mod_ref[scale_row:scale_row + 1, :])) + mod_ref[shift_row:shift_row + 1, :]


def _inproj_even_kernel(x_ref, mod_ref, nw_ref, w_ref, o_ref, f_ref):
    h = _norm_mod(x_ref[...], nw_ref[...], mod_ref, 0, 1)
    res = jnp.dot(h.astype(BF16), w_ref[...], preferred_element_type=F32)
    o_ref[...] = res.astype(BF16)
    f_ref[...] = res[:, 512:1024]


def _log_sigmoid(z):
    return jnp.minimum(z, 0.0) - jnp.log1p(jnp.exp(-jnp.abs(z)))


def _inproj_odd_kernel(x_ref, mod_ref, nw_ref, w_ref, wa_ref, wa2_ref, ba2_ref, o_ref, la_ref):
    h = _norm_mod(x_ref[...], nw_ref[...], mod_ref, 0, 1).astype(BF16)
    o_ref[...] = jnp.dot(h, w_ref[...], preferred_element_type=F32).astype(BF16)
    ga = jnp.dot(h, wa_ref[...], preferred_element_type=F32)
    z = jnp.dot(ga.astype(BF16), wa2_ref[...], preferred_element_type=F32) + ba2_ref[...]
    la_ref[...] = _log_sigmoid(z) * (1.0 / GLA_GATE_NORM)


def _inproj(x2, mod4, layer, seq, nw, weights, kernel_fn, n_main, extra_specs, extra_args):
    t, d = x2.shape
    tm = min(256, seq)
    per_b = seq // tm
    return pl.pallas_call(
        kernel_fn,
        grid=(t // tm,),
        in_specs=[pl.BlockSpec((tm, d), lambda i: (i, 0)),
                  pl.BlockSpec((None, None, 6, d), lambda i: (layer, i // per_b, 0, 0)),
                  pl.BlockSpec((1, d), lambda i: (0, 0)),
                  pl.BlockSpec((d, n_main), lambda i: (0, 0))] + extra_specs,
        out_specs=[pl.BlockSpec((tm, n_main), lambda i: (i, 0)),
                   pl.BlockSpec((tm, 512), lambda i: (i, 0))],
        out_shape=[jax.ShapeDtypeStruct((t, n_main), BF16),
                   jax.ShapeDtypeStruct((t, 512), F32)],
        compiler_params=_cparams(("parallel",)),
        name=f"inproj_l{layer}",
    )(x2, mod4, nw.reshape(1, d), weights, *extra_args)


def _causal_mask():
    r = lax.broadcasted_iota(jnp.int32, (CHUNK, CHUNK), 0)
    c = lax.broadcasted_iota(jnp.int32, (CHUNK, CHUNK), 1)
    return r >= c


def _dot_nt(a, b):
    return lax.dot_general(a, b, (((1,), (1,)), ((), ())), preferred_element_type=F32)


def _dot_tn(a, b):
    return lax.dot_general(a, b, (((0,), (0,)), ((), ())), preferred_element_type=F32)


def _head_out(o, g, nw, n_norm):
    ms = jnp.sum(o * o, axis=-1, keepdims=True) * (1.0 / n_norm)
    return o * lax.rsqrt(ms + EPS) * nw * _silu(g)


def _cumsum_rows(lg):
    tril = _causal_mask().astype(F32)
    return jnp.dot(tril, lg, preferred_element_type=F32, precision=HIGHEST)


def _decay_heads(q, k, v, b, st_ref, st_base, dv):
    b_mid = b[CHUNK // 2 - 1:CHUNK // 2, :]
    b_last = b[CHUNK - 1:CHUNK, :]
    qe = q * jnp.exp(b - b_mid)
    ke = k * jnp.exp(b_mid - b)
    qs = qe.astype(BF16)
    ks = ke.astype(BF16)
    qi = (qe * jnp.exp(b_mid)).astype(BF16)
    kd = (ke * jnp.exp(b_last - b_mid)).astype(BF16)
    dec = jnp.exp(b_last)
    mask = _causal_mask()
    outs = []
    for h in range(N_HEADS):
        kl = slice(h * HEAD_DIM, (h + 1) * HEAD_DIM)
        vh = v[:, h * dv:(h + 1) * dv]
        scores = jnp.where(mask, _dot_nt(qs[:, kl], ks[:, kl]), 0.0)
        o = jnp.dot(scores.astype(BF16), vh, preferred_element_type=F32)
        st = st_ref[st_base + h]
        o = o + _dot_nt(qi[:, kl], st.astype(BF16))
        st_ref[st_base + h] = dec[:, kl] * st + _dot_tn(vh, kd[:, kl])
        outs.append(o)
    return outs


def _even_mixer_kernel(proj_ref, hf_ref, lbl_ref, hgnw_ref, retnw_ref, cos_ref, sin_ref, o_ref, st_ref,
                       *, layer, n_chunks):
    hd, w = HEAD_DIM, N_HEADS * HEAD_DIM

    @pl.when(pl.program_id(1) == 0)
    def _():
        st_ref[...] = jnp.zeros_like(st_ref)

    lbl = lbl_ref[...]
    e = jnp.exp(lbl - jnp.max(lbl, axis=0, keepdims=True))
    p = e / jnp.sum(e, axis=0, keepdims=True)
    lb = jnp.sum(p[:layer + 1], axis=0, keepdims=True)
    hgnw = hgnw_ref[...]
    retnw = retnw_ref[...]
    headf = (lax.broadcasted_iota(jnp.int32, (1, w), 1) // hd).astype(F32)
    lg = jnp.log1p(-jnp.exp2(-5.0 - headf))
    pos = lax.broadcasted_iota(jnp.int32, (CHUNK, w), 0).astype(F32)
    e_q = jnp.exp((pos + 1.0) * lg)
    e_k = jnp.exp((CHUNK - 1.0 - pos) * lg)
    e_c = jnp.exp(CHUNK * lg)
    r = lax.broadcasted_iota(jnp.int32, (CHUNK, CHUNK), 0)
    cc = lax.broadcasted_iota(jnp.int32, (CHUNK, CHUNK), 1)
    dt = (r - cc).astype(F32)
    dmasks = [jnp.where(r >= cc, jnp.exp(dt * lg[:, h * hd:h * hd + CHUNK]), 0.0) for h in range(N_HEADS)]

    def rot(xx, cs, sn):
        return xx * cs + pltpu.roll(xx, hd // 2, 1) * sn

    def body(c, carry):
        sl = pl.ds(pl.multiple_of(c * CHUNK, CHUNK), CHUNK)
        f = lb + (1.0 - lb) * _sigmoid(hf_ref[sl, :])
        b = _cumsum_rows(jnp.log(f))
        q = _sigmoid(proj_ref[sl, 0:w].astype(F32))
        outs = _decay_heads(q, 1.0 - f, proj_ref[sl, 2 * w:3 * w], b, st_ref, 0, hd)
        for h, o in enumerate(outs):
            cl = slice(h * hd, (h + 1) * hd)
            g = proj_ref[sl, 3 * w + h * hd:3 * w + (h + 1) * hd].astype(F32)
            o_ref[sl, cl] = _head_out(o, g, hgnw[:, cl], hd).astype(BF16)
        cs = cos_ref[sl, :]
        sn = sin_ref[sl, :]
        for h in range(N_HEADS):
            cl = slice(h * hd, (h + 1) * hd)
            qh = rot(proj_ref[sl, 4 * w + h * hd:4 * w + (h + 1) * hd].astype(F32), cs, sn)
            kh = rot(proj_ref[sl, 5 * w + h * hd:5 * w + (h + 1) * hd].astype(F32), cs, sn) * (hd ** -0.5)
            vh = proj_ref[sl, 6 * w + h * hd:6 * w + (h + 1) * hd]
            scores = _dot_nt(qh.astype(BF16), kh.astype(BF16)) * dmasks[h]
            o = jnp.dot(scores.astype(BF16), vh, preferred_element_type=F32)
            st = st_ref[N_HEADS + h]
            o = o + _dot_nt((qh * e_q[:, cl]).astype(BF16), st.astype(BF16))
            st_ref[N_HEADS + h] = e_c[:, cl] * st + _dot_tn(vh, (kh * e_k[:, cl]).astype(BF16))
            g = proj_ref[sl, 7 * w + h * hd:7 * w + (h + 1) * hd].astype(F32)
            o_ref[sl, w + h * hd:w + (h + 1) * hd] = _head_out(o, g, retnw[:, cl], hd).astype(BF16)
        return carry

    lax.fori_loop(0, n_chunks, body, 0)


def _gla_mixer_kernel(proj_ref, la_ref, nw_ref, o_ref, st_ref, *, n_chunks):
    hd, w = HEAD_DIM, N_HEADS * HEAD_DIM
    dv = 2 * hd

    @pl.when(pl.program_id(1) == 0)
    def _():
        st_ref[...] = jnp.zeros_like(st_ref)

    nw = nw_ref[...]

    def body(c, carry):
        sl = pl.ds(pl.multiple_of(c * CHUNK, CHUNK), CHUNK)
        b = _cumsum_rows(la_ref[sl, :])
        q = proj_ref[sl, 0:w].astype(F32) * (hd ** -0.5)
        k = proj_ref[sl, w:2 * w].astype(F32)
        outs = _decay_heads(q, k, proj_ref[sl, 2 * w:2 * w + N_HEADS * dv], b, st_ref, 0, dv)
        for h, o in enumerate(outs):
            cl = slice(h * dv, (h + 1) * dv)
            g = proj_ref[sl, 4 * w + h * dv:4 * w + (h + 1) * dv].astype(F32)
            o_ref[sl, cl] = _head_out(o, g, nw[:, cl], dv).astype(BF16)
        return carry

    lax.fori_loop(0, n_chunks, body, 0)


def _mixer_call(kernel_fn, bsz, seq, ts, in_specs, n_states, dv, name, args):
    d_out = N_HEADS * 2 * HEAD_DIM
    return pl.pallas_call(
        kernel_fn,
        grid=(bsz, seq // ts),
        in_specs=in_specs,
        out_specs=pl.BlockSpec((ts, d_out), lambda b, s: (b * (seq // ts) + s, 0)),
        out_shape=jax.ShapeDtypeStruct((bsz * seq, d_out), BF16),
        scratch_shapes=[pltpu.VMEM((n_states, dv, HEAD_DIM), F32)],
        compiler_params=_cparams(("parallel", "arbitrary")),
        name=name,
    )(*args)


def _even_mixers(proj, hf, lb_logits, hg_nw, ret_nw, bsz, seq, layer):
    ts = min(MIX_SEQ_TILE, seq)
    per_b = seq // ts
    hd = HEAD_DIM
    rows = lambda width: pl.BlockSpec((ts, width), lambda b, s: (b * per_b + s, 0))
    full = lambda a: pl.BlockSpec(a.shape, lambda b, s: (0,) * a.ndim)
    inv = ROPE_BASE ** (-jnp.arange(0, hd, 2, dtype=F32) / hd)
    ang = jnp.arange(seq, dtype=F32)[:, None] * inv[None, :]
    cos2 = jnp.concatenate([jnp.cos(ang), jnp.cos(ang)], axis=-1)
    sin2 = jnp.concatenate([-jnp.sin(ang), jnp.sin(ang)], axis=-1)
    tab = pl.BlockSpec((ts, hd), lambda b, s: (s, 0))
    hg_nw, ret_nw = hg_nw.reshape(1, -1), ret_nw.reshape(1, -1)
    return _mixer_call(
        functools.partial(_even_mixer_kernel, layer=layer, n_chunks=ts // CHUNK), bsz, seq, ts,
        [rows(proj.shape[1]), rows(hf.shape[1]), full(lb_logits), full(hg_nw), full(ret_nw), tab, tab],
        2 * N_HEADS, hd, f"even_mixer_l{layer}",
        (proj, hf, lb_logits, hg_nw, ret_nw, cos2, sin2))


def _odd_mixer(proj, log_a, gla_nw, bsz, seq, layer):
    ts = min(MIX_SEQ_TILE, seq)
    per_b = seq // ts
    rows = lambda width: pl.BlockSpec((ts, width), lambda b, s: (b * per_b + s, 0))
    gla_nw = gla_nw.reshape(1, -1)
    return _mixer_call(
        functools.partial(_gla_mixer_kernel, n_chunks=ts // CHUNK), bsz, seq, ts,
        [rows(proj.shape[1]), rows(log_a.shape[1]), pl.BlockSpec(gla_nw.shape, lambda b, s: (0, 0))],
        N_HEADS, 2 * HEAD_DIM, f"gla_mixer_l{layer}",
        (proj, log_a, gla_nw))


def _route(h2, wr, br):
    neg = -jnp.inf
    h_hi = h2.astype(BF16)
    h_lo = (h2 - h_hi.astype(F32)).astype(BF16)
    w_hi = wr.astype(BF16)
    w_lo = (wr - w_hi.astype(F32)).astype(BF16)
    logits = (jnp.dot(h_hi, w_hi, preferred_element_type=F32) + jnp.dot(h_lo, w_hi, preferred_element_type=F32)
              + jnp.dot(h_hi, w_lo, preferred_element_type=F32))
    lane = lax.broadcasted_iota(jnp.int32, logits.shape, 1)
    valid = lane < N_EXPERTS
    lm = jnp.where(valid, logits, neg)
    ex = jnp.where(valid, jnp.exp(lm - jnp.max(lm, axis=-1, keepdims=True)), 0.0)
    probs = ex / jnp.sum(ex, axis=-1, keepdims=True)
    sel = jnp.where(valid, probs + br, neg)
    grp = lane // EXPERTS_PER_GROUP
    gs = [jnp.max(jnp.where(grp == g, sel, neg), axis=-1, keepdims=True) for g in range(N_GROUPS)]
    best = functools.reduce(jnp.maximum, gs)
    gidx = jnp.full(best.shape, N_GROUPS - 1, jnp.int32)
    for g in range(N_GROUPS - 2, -1, -1):
        gidx = jnp.where(gs[g] == best, g, gidx)
    masked = jnp.where(grp == gidx, sel, neg)
    m1 = jnp.max(masked, axis=-1, keepdims=True)
    i1 = jnp.min(jnp.where(masked == m1, lane, LANES), axis=-1, keepdims=True)
    masked2 = jnp.where(lane == i1, neg, masked)
    m2 = jnp.max(masked2, axis=-1, keepdims=True)
    i2 = jnp.min(jnp.where(masked2 == m2, lane, LANES), axis=-1, keepdims=True)
    p1 = jnp.sum(jnp.where(lane == i1, probs, 0.0), axis=-1, keepdims=True)
    p2 = jnp.sum(jnp.where(lane == i2, probs, 0.0), axis=-1, keepdims=True)
    den = p1 + p2
    return jnp.where(lane == i1, p1 / den, 0.0) + jnp.where(lane == i2, p2 / den, 0.0)


def _outproj_kernel(*refs, n_o):
    o_refs = refs[:n_o]
    w_refs = refs[n_o:2 * n_o]
    x_ref, mod_ref, nw_ref, wr_ref, br_ref, x1_ref, h2_ref, cw_ref = refs[2 * n_o:]
    y = jnp.dot(o_refs[0][...], w_refs[0][...], preferred_element_type=F32)
    for o_ref, w_ref in zip(o_refs[1:], w_refs[1:]):
        y = y + jnp.dot(o_ref[...], w_ref[...], preferred_element_type=F32)
    x1 = x_ref[...] + mod_ref[2:3, :] * y
    x1_ref[...] = x1
    h2 = _norm_mod(x1, nw_ref[...], mod_ref, 3, 4)
    h2_ref[...] = h2.astype(BF16)
    cw_ref[...] = _route(h2, wr_ref[...], br_ref[...])


def _outproj_route(o_list, w_list, x2, mod4, layer, seq, ffn_nw, wr_pad, br_pad):
    t, d = x2.shape
    tm = min(256, seq)
    per_b = seq // tm
    n_o = len(o_list)
    full = lambda a: pl.BlockSpec(a.shape, lambda i: (0,) * a.ndim)
    in_specs = ([pl.BlockSpec((tm, o.shape[1]), lambda i: (i, 0)) for o in o_list]
                + [full(w) for w in w_list]
                + [pl.BlockSpec((tm, d), lambda i: (i, 0)),
                   pl.BlockSpec((None, None, 6, d), lambda i: (layer, i // per_b, 0, 0)),
                   pl.BlockSpec((1, d), lambda i: (0, 0)),
                   full(wr_pad), full(br_pad)])
    return pl.pallas_call(
        functools.partial(_outproj_kernel, n_o=n_o),
        grid=(t // tm,),
        in_specs=in_specs,
        out_specs=[pl.BlockSpec((tm, d), lambda i: (i, 0)),
                   pl.BlockSpec((tm, d), lambda i: (i, 0)),
                   pl.BlockSpec((tm, LANES), lambda i: (i, 0))],
        out_shape=[jax.ShapeDtypeStruct((t, d), F32),
                   jax.ShapeDtypeStruct((t, d), BF16),
                   jax.ShapeDtypeStruct((t, LANES), F32)],
        compiler_params=_cparams(("parallel",)),
        name=f"outproj_route_l{layer}",
    )(*o_list, *w_list, x2, mod4, ffn_nw.reshape(1, d), wr_pad, br_pad)


def _moe_kernel(h_ref, cw_ref, wg_ref, wu_ref, wd_ref, x_ref, mod_ref, fw_ref, o_ref, acc_ref, *, final):
    e = pl.program_id(1)

    @pl.when(e == 0)
    def _():
        acc_ref[...] = jnp.zeros_like(acc_ref)

    h = h_ref[...]
    g = jnp.dot(h, wg_ref[0], preferred_element_type=F32)
    u = jnp.dot(h, wu_ref[0], preferred_element_type=F32)
    y = jnp.dot((_silu(g) * u).astype(BF16), wd_ref[0], preferred_element_type=F32)
    cw = cw_ref[...]
    lane = lax.broadcasted_iota(jnp.int32, cw.shape, 1)
    acc_ref[...] += jnp.sum(jnp.where(lane == e, cw, 0.0), axis=-1, keepdims=True) * y

    @pl.when(e == N_EXPERTS - 1)
    def _():
        xn = x_ref[...] + mod_ref[5:6, :] * acc_ref[...]
        if final:
            ms = jnp.mean(xn * xn, axis=-1, keepdims=True)
            xn = xn * lax.rsqrt(ms + EPS) * fw_ref[...]
        o_ref[...] = xn


def _moe(h2, cw, wg, wu, wd, x1, mod4, layer, seq, final_w, final):
    t, d = x1.shape
    de = wg.shape[-1]
    tm = min(1024, seq)
    per_b = seq // tm
    return pl.pallas_call(
        functools.partial(_moe_kernel, final=final),
        grid=(t // tm, N_EXPERTS),
        in_specs=[pl.BlockSpec((tm, d), lambda i, e: (i, 0)),
                  pl.BlockSpec((tm, LANES), lambda i, e: (i, 0)),
                  pl.BlockSpec((1, d, de), lambda i, e: (e, 0, 0)),
                  pl.BlockSpec((1, d, de), lambda i, e: (e, 0, 0)),
                  pl.BlockSpec((1, de, d), lambda i, e: (e, 0, 0)),
                  pl.BlockSpec((tm, d), lambda i, e: (i, 0)),
                  pl.BlockSpec((None, None, 6, d), lambda i, e: (layer, i // per_b, 0, 0)),
                  pl.BlockSpec((1, d), lambda i, e: (0, 0))],
        out_specs=pl.BlockSpec((tm, d), lambda i, e: (i, 0)),
        out_shape=jax.ShapeDtypeStruct((t, d), F32),
        scratch_shapes=[pltpu.VMEM((tm, d), F32)],
        compiler_params=_cparams(("parallel", "arbitrary")),
        name=f"moe_l{layer}",
    )(h2, cw, wg, wu, wd, x1, mod4, final_w.reshape(1, d))


def kernel(x, c, w_ada, b_ada, norm_mix_w, norm_ffn_w, w_in_even, hg_lb_logits, hg_norm_w, ret_norm_w,
           w_out_even, w_in_odd, w_gla_a2, b_gla_a2, gla_norm_w, w_out_odd, w_router, b_router,
           w_gate, w_up, w_down, final_norm_w):
    bsz, seq, d = x.shape
    depth = w_ada.shape[0]
    x2 = x.reshape(bsz * seq, d)
    mod4 = _adaln_mod(c, w_ada, b_ada).reshape(depth, bsz, 6, d)
    wr_pad = jnp.pad(w_router, ((0, 0), (0, LANES - N_EXPERTS)))
    br_pad = jnp.pad(b_router, (0, LANES - N_EXPERTS)).reshape(1, LANES)
    for l in range(depth):
        j = l // 2
        if l % 2 == 0:
            proj, hf = _inproj(x2, mod4, l, seq, norm_mix_w[l], w_in_even[j].astype(BF16),
                               _inproj_even_kernel, w_in_even.shape[-1], [], [])
            o = _even_mixers(proj, hf, hg_lb_logits, hg_norm_w[j], ret_norm_w[j], bsz, seq, l)
            o_list, w_list = [o], [w_out_even[j].astype(BF16)]
        else:
            n_main = w_in_odd.shape[-1] - GLA_RANK
            wi = w_in_odd[j]
            wa = jnp.pad(wi[:, n_main:], ((0, 0), (0, LANES - GLA_RANK))).astype(BF16)
            wa2 = jnp.pad(w_gla_a2[j], ((0, LANES - GLA_RANK), (0, 0))).astype(BF16)
            full = lambda a: pl.BlockSpec(a.shape, lambda i: (0,) * a.ndim)
            ba2 = b_gla_a2[j].reshape(1, -1)
            proj, log_a = _inproj(x2, mod4, l, seq, norm_mix_w[l], wi[:, :n_main].astype(BF16),
                                  _inproj_odd_kernel, n_main, [full(wa), full(wa2), full(ba2)], [wa, wa2, ba2])
            o_list, w_list = [_odd_mixer(proj, log_a, gla_norm_w[j], bsz, seq, l)], [w_out_odd[j].astype(BF16)]
        x1, h2, cw = _outproj_route(o_list, w_list, x2, mod4, l, seq, norm_ffn_w[l], wr_pad, br_pad)
        x2 = _moe(h2, cw, w_gate[l].astype(BF16), w_up[l].astype(BF16), w_down[l].astype(BF16),
                  x1, mod4, l, seq, final_norm_w, final=(l == depth - 1))
    return x2.reshape(bsz, seq, d)
```

```python
import functools

import jax
import jax.numpy as jnp
from jax import lax
from jax.experimental import pallas as pl
from jax.experimental.pallas import tpu as pltpu

F32 = jnp.float32
BF16 = jnp.bfloat16
HIGHEST = lax.Precision.HIGHEST

EPS = 1e-6
CHUNK = 64
HEAD_DIM = 128
N_HEADS = 4
N_EXPERTS = 16
N_GROUPS = 4
EXPERTS_PER_GROUP = N_EXPERTS // N_GROUPS
GLA_RANK = 16
GLA_GATE_NORM = 16.0
ROPE_BASE = 10000.0
LANES = 128
VMEM_LIMIT = 52 * 1024 * 1024
SUBLANES = 8
MIX_SEQ_TILE = 512
TOP_K = 2
ROW_GROUP = 16
MOE_TOKEN_TILE = 512
FFN_ROW_TILE = 512
PERM_BLOCK = 256


def _cparams(sem):
    return pltpu.CompilerParams(dimension_semantics=sem, vmem_limit_bytes=VMEM_LIMIT)


def _sigmoid(z):
    return 0.5 * jnp.tanh(0.5 * z) + 0.5


def _silu(z):
    return z * _sigmoid(z)


def _mod_kernel(c_ref, w_ref, b_ref, o_ref):
    c = c_ref[...]
    o_ref[0] = jnp.dot(_silu(c), w_ref[0], preferred_element_type=F32, precision=HIGHEST) + b_ref[0]


def _adaln_mod(c, w_ada, b_ada):
    depth, d, n = w_ada.shape
    bsz = c.shape[0]
    tn = 1536
    return pl.pallas_call(
        _mod_kernel,
        grid=(depth, n // tn),
        in_specs=[pl.BlockSpec((bsz, d), lambda l, j: (0, 0)),
                  pl.BlockSpec((1, d, tn), lambda l, j: (l, 0, j)),
                  pl.BlockSpec((1, 1, tn), lambda l, j: (l, 0, j))],
        out_specs=pl.BlockSpec((1, bsz, tn), lambda l, j: (l, 0, j)),
        out_shape=jax.ShapeDtypeStruct((depth, bsz, n), F32),
        compiler_params=_cparams(("arbitrary", "arbitrary")),
        name="adaln_mod",
    )(c, w_ada, b_ada.reshape(depth, 1, n))


def _norm_mod(x, nw, mod_ref, shift_row, scale_row):
    ms = jnp.mean(x * x, axis=-1, keepdims=True)
    return x * lax.rsqrt(ms + EPS) * (nw * (1.0 + mod_ref[scale_row:scale_row + 1, :])) + mod_ref[shift_row:shift_row + 1, :]


def _inproj_even_kernel(x_ref, mod_ref, nw_ref, w_ref, o_ref, f_ref):
    h = _norm_mod(x_ref[...], nw_ref[...], mod_ref, 0, 1)
    res = jnp.dot(h.astype(BF16), w_ref[...], preferred_element_type=F32)
    o_ref[...] = res.astype(BF16)
    f_ref[...] = res[:, 512:1024]


def _log_sigmoid(z):
    return jnp.minimum(z, 0.0) - jnp.log1p(jnp.exp(-jnp.abs(z)))


def _inproj_odd_kernel(x_ref, mod_ref, nw_ref, w_ref, wa_ref, wa2_ref, ba2_ref, o_ref, la_ref):
    h = _norm_mod(x_ref[...], nw_ref[...], mod_ref, 0, 1).astype(BF16)
    o_ref[...] = jnp.dot(h, w_ref[...], preferred_element_type=F32).astype(BF16)
    ga = jnp.dot(h, wa_ref[...], preferred_element_type=F32)
    z = jnp.dot(ga.astype(BF16), wa2_ref[...], preferred_element_type=F32) + ba2_ref[...]
    la_ref[...] = _log_sigmoid(z) * (1.0 / GLA_GATE_NORM)


def _inproj(x2, mod4, layer, seq, nw, weights, kernel_fn, n_main, extra_specs, extra_args):
    t, d = x2.shape
    tm = min(256, seq)
    per_b = seq // tm
    return pl.pallas_call(
        kernel_fn,
        grid=(t // tm,),
        in_specs=[pl.BlockSpec((tm, d), lambda i: (i, 0)),
                  pl.BlockSpec((None, None, 6, d), lambda i: (layer, i // per_b, 0, 0)),
                  pl.BlockSpec((1, d), lambda i: (0, 0)),
                  pl.BlockSpec((d, n_main), lambda i: (0, 0))] + extra_specs,
        out_specs=[pl.BlockSpec((tm, n_main), lambda i: (i, 0)),
                   pl.BlockSpec((tm, 512), lambda i: (i, 0))],
        out_shape=[jax.ShapeDtypeStruct((t, n_main), BF16),
                   jax.ShapeDtypeStruct((t, 512), F32)],
        compiler_params=_cparams(("parallel",)),
        name=f"inproj_l{layer}",
    )(x2, mod4, nw.reshape(1, d), weights, *extra_args)


def _causal_mask():
    r = lax.broadcasted_iota(jnp.int32, (CHUNK, CHUNK), 0)
    c = lax.broadcasted_iota(jnp.int32, (CHUNK, CHUNK), 1)
    return r >= c


def _dot_nt(a, b):
    return lax.dot_general(a, b, (((1,), (1,)), ((), ())), preferred_element_type=F32)


def _dot_tn(a, b):
    return lax.dot_general(a, b, (((0,), (0,)), ((), ())), preferred_element_type=F32)


def _head_out(o, g, nw, n_norm):
    ms = jnp.sum(o * o, axis=-1, keepdims=True) * (1.0 / n_norm)
    return o * lax.rsqrt(ms + EPS) * nw * _silu(g)


def _cumsum_rows(lg):
    tril = _causal_mask().astype(F32)
    return jnp.dot(tril, lg, preferred_element_type=F32, precision=HIGHEST)


def _decay_heads(q, k, v, b, st_ref, st_base, dv):
    b_mid = b[CHUNK // 2 - 1:CHUNK // 2, :]
    b_last = b[CHUNK - 1:CHUNK, :]
    qe = q * jnp.exp(b - b_mid)
    ke = k * jnp.exp(b_mid - b)
    qs = qe.astype(BF16)
    ks = ke.astype(BF16)
    qi = (qe * jnp.exp(b_mid)).astype(BF16)
    kd = (ke * jnp.exp(b_last - b_mid)).astype(BF16)
    dec = jnp.exp(b_last)
    mask = _causal_mask()
    outs = []
    for h in range(N_HEADS):
        kl = slice(h * HEAD_DIM, (h + 1) * HEAD_DIM)
        vh = v[:, h * dv:(h + 1) * dv]
        scores = jnp.where(mask, _dot_nt(qs[:, kl], ks[:, kl]), 0.0)
        o = jnp.dot(scores.astype(BF16), vh, preferred_element_type=F32)
        st = st_ref[st_base + h]
        o = o + _dot_nt(qi[:, kl], st.astype(BF16))
        st_ref[st_base + h] = dec[:, kl] * st + _dot_tn(vh, kd[:, kl])
        outs.append(o)
    return outs


def _even_mixer_kernel(proj_ref, hf_ref, lbl_ref, hgnw_ref, retnw_ref, cos_ref, sin_ref, o_ref, st_ref,
                       *, layer, n_chunks):
    hd, w = HEAD_DIM, N_HEADS * HEAD_DIM

    @pl.when(pl.program_id(1) == 0)
    def _():
        st_ref[...] = jnp.zeros_like(st_ref)

    lbl = lbl_ref[...]
    e = jnp.exp(lbl - jnp.max(lbl, axis=0, keepdims=True))
    p = e / jnp.sum(e, axis=0, keepdims=True)
    lb = jnp.sum(p[:layer + 1], axis=0, keepdims=True)
    hgnw = hgnw_ref[...]
    retnw = retnw_ref[...]
    headf = (lax.broadcasted_iota(jnp.int32, (1, w), 1) // hd).astype(F32)
    lg = jnp.log1p(-jnp.exp2(-5.0 - headf))
    pos = lax.broadcasted_iota(jnp.int32, (CHUNK, w), 0).astype(F32)
    e_q = jnp.exp((pos + 1.0) * lg)
    e_k = jnp.exp((CHUNK - 1.0 - pos) * lg)
    e_c = jnp.exp(CHUNK * lg)
    r = lax.broadcasted_iota(jnp.int32, (CHUNK, CHUNK), 0)
    cc = lax.broadcasted_iota(jnp.int32, (CHUNK, CHUNK), 1)
    dt = (r - cc).astype(F32)
    dmasks = [jnp.where(r >= cc, jnp.exp(dt * lg[:, h * hd:h * hd + CHUNK]), 0.0) for h in range(N_HEADS)]

    def rot(xx, cs, sn):
        return xx * cs + pltpu.roll(xx, hd // 2, 1) * sn

    def body(c, carry):
        sl = pl.ds(pl.multiple_of(c * CHUNK, CHUNK), CHUNK)
        f = lb + (1.0 - lb) * _sigmoid(hf_ref[sl, :])
        b = _cumsum_rows(jnp.log(f))
        q = _sigmoid(proj_ref[sl, 0:w].astype(F32))
        outs = _decay_heads(q, 1.0 - f, proj_ref[sl, 2 * w:3 * w], b, st_ref, 0, hd)
        for h, o in enumerate(outs):
            cl = slice(h * hd, (h + 1) * hd)
            g = proj_ref[sl, 3 * w + h * hd:3 * w + (h + 1) * hd].astype(F32)
            o_ref[sl, cl] = _head_out(o, g, hgnw[:, cl], hd).astype(BF16)
        cs = cos_ref[sl, :]
        sn = sin_ref[sl, :]
        for h in range(N_HEADS):
            cl = slice(h * hd, (h + 1) * hd)
            qh = rot(proj_ref[sl, 4 * w + h * hd:4 * w + (h + 1) * hd].astype(F32), cs, sn)
            kh = rot(proj_ref[sl, 5 * w + h * hd:5 * w + (h + 1) * hd].astype(F32), cs, sn) * (hd ** -0.5)
            vh = proj_ref[sl, 6 * w + h * hd:6 * w + (h + 1) * hd]
            scores = _dot_nt(qh.astype(BF16), kh.astype(BF16)) * dmasks[h]
            o = jnp.dot(scores.astype(BF16), vh, preferred_element_type=F32)
            st = st_ref[N_HEADS + h]
            o = o + _dot_nt((qh * e_q[:, cl]).astype(BF16), st.astype(BF16))
            st_ref[N_HEADS + h] = e_c[:, cl] * st + _dot_tn(vh, (kh * e_k[:, cl]).astype(BF16))
            g = proj_ref[sl, 7 * w + h * hd:7 * w + (h + 1) * hd].astype(F32)
            o_ref[sl, w + h * hd:w + (h + 1) * hd] = _head_out(o, g, retnw[:, cl], hd).astype(BF16)
        return carry

    lax.fori_loop(0, n_chunks, body, 0)


def _gla_mixer_kernel(proj_ref, la_ref, nw_ref, o_ref, st_ref, *, n_chunks):
    hd, w = HEAD_DIM, N_HEADS * HEAD_DIM
    dv = 2 * hd

    @pl.when(pl.program_id(1) == 0)
    def _():
        st_ref[...] = jnp.zeros_like(st_ref)

    nw = nw_ref[...]

    def body(c, carry):
        sl = pl.ds(pl.multiple_of(c * CHUNK, CHUNK), CHUNK)
        b = _cumsum_rows(la_ref[sl, :])
        q = proj_ref[sl, 0:w].astype(F32) * (hd ** -0.5)
        k = proj_ref[sl, w:2 * w].astype(F32)
        outs = _decay_heads(q, k, proj_ref[sl, 2 * w:2 * w + N_HEADS * dv], b, st_ref, 0, dv)
        for h, o in enumerate(outs):
            cl = slice(h * dv, (h + 1) * dv)
            g = proj_ref[sl, 4 * w + h * dv:4 * w + (h + 1) * dv].astype(F32)
            o_ref[sl, cl] = _head_out(o, g, nw[:, cl], dv).astype(BF16)
        return carry

    lax.fori_loop(0, n_chunks, body, 0)


def _mixer_call(kernel_fn, bsz, seq, ts, in_specs, n_states, dv, name, args):
    d_out = N_HEADS * 2 * HEAD_DIM
    return pl.pallas_call(
        kernel_fn,
        grid=(bsz, seq // ts),
        in_specs=in_specs,
        out_specs=pl.BlockSpec((ts, d_out), lambda b, s: (b * (seq // ts) + s, 0)),
        out_shape=jax.ShapeDtypeStruct((bsz * seq, d_out), BF16),
        scratch_shapes=[pltpu.VMEM((n_states, dv, HEAD_DIM), F32)],
        compiler_params=_cparams(("parallel", "arbitrary")),
        name=name,
    )(*args)


def _even_mixers(proj, hf, lb_logits, hg_nw, ret_nw, bsz, seq, layer):
    ts = min(MIX_SEQ_TILE, seq)
    per_b = seq // ts
    hd = HEAD_DIM
    rows = lambda width: pl.BlockSpec((ts, width), lambda b, s: (b * per_b + s, 0))
    full = lambda a: pl.BlockSpec(a.shape, lambda b, s: (0,) * a.ndim)
    inv = ROPE_BASE ** (-jnp.arange(0, hd, 2, dtype=F32) / hd)
    ang = jnp.arange(seq, dtype=F32)[:, None] * inv[None, :]
    cos2 = jnp.concatenate([jnp.cos(ang), jnp.cos(ang)], axis=-1)
    sin2 = jnp.concatenate([-jnp.sin(ang), jnp.sin(ang)], axis=-1)
    tab = pl.BlockSpec((ts, hd), lambda b, s: (s, 0))
    hg_nw, ret_nw = hg_nw.reshape(1, -1), ret_nw.reshape(1, -1)
    return _mixer_call(
        functools.partial(_even_mixer_kernel, layer=layer, n_chunks=ts // CHUNK), bsz, seq, ts,
        [rows(proj.shape[1]), rows(hf.shape[1]), full(lb_logits), full(hg_nw), full(ret_nw), tab, tab],
        2 * N_HEADS, hd, f"even_mixer_l{layer}",
        (proj, hf, lb_logits, hg_nw, ret_nw, cos2, sin2))


def _odd_mixer(proj, log_a, gla_nw, bsz, seq, layer):
    ts = min(MIX_SEQ_TILE, seq)
    per_b = seq // ts
    rows = lambda width: pl.BlockSpec((ts, width), lambda b, s: (b * per_b + s, 0))
    gla_nw = gla_nw.reshape(1, -1)
    return _mixer_call(
        functools.partial(_gla_mixer_kernel, n_chunks=ts // CHUNK), bsz, seq, ts,
        [rows(proj.shape[1]), rows(log_a.shape[1]), pl.BlockSpec(gla_nw.shape, lambda b, s: (0, 0))],
        N_HEADS, 2 * HEAD_DIM, f"gla_mixer_l{layer}",
        (proj, log_a, gla_nw))


def _lane_pick(lane, cols):
    out = jnp.zeros(lane.shape, F32)
    for j, col in reversed(list(enumerate(cols))):
        out = jnp.where(lane == j, col, out)
    return out


def _route(h2, wr, br):
    neg = -jnp.inf
    h_hi = h2.astype(BF16)
    h_lo = (h2 - h_hi.astype(F32)).astype(BF16)
    w_hi = wr.astype(BF16)
    w_lo = (wr - w_hi.astype(F32)).astype(BF16)
    logits = (jnp.dot(h_hi, w_hi, preferred_element_type=F32) + jnp.dot(h_lo, w_hi, preferred_element_type=F32)
              + jnp.dot(h_hi, w_lo, preferred_element_type=F32))
    tm = logits.shape[0]
    lane = lax.broadcasted_iota(jnp.int32, logits.shape, 1)
    valid = lane < N_EXPERTS
    lm = jnp.where(valid, logits, neg)
    ex = jnp.where(valid, jnp.exp(lm - jnp.max(lm, axis=-1, keepdims=True)), 0.0)
    probs = ex / jnp.sum(ex, axis=-1, keepdims=True)
    sel = jnp.where(valid, probs + br, neg)
    grp = lane // EXPERTS_PER_GROUP
    gs = [jnp.max(jnp.where(grp == g, sel, neg), axis=-1, keepdims=True) for g in range(N_GROUPS)]
    best = functools.reduce(jnp.maximum, gs)
    gidx = jnp.full(best.shape, N_GROUPS - 1, jnp.int32)
    for g in range(N_GROUPS - 2, -1, -1):
        gidx = jnp.where(gs[g] == best, g, gidx)
    masked = jnp.where(grp == gidx, sel, neg)
    m1 = jnp.max(masked, axis=-1, keepdims=True)
    i1 = jnp.min(jnp.where(masked == m1, lane, LANES), axis=-1, keepdims=True)
    masked2 = jnp.where(lane == i1, neg, masked)
    m2 = jnp.max(masked2, axis=-1, keepdims=True)
    i2 = jnp.min(jnp.where(masked2 == m2, lane, LANES), axis=-1, keepdims=True)
    p1 = jnp.sum(jnp.where(lane == i1, probs, 0.0), axis=-1, keepdims=True)
    p2 = jnp.sum(jnp.where(lane == i2, probs, 0.0), axis=-1, keepdims=True)
    den = p1 + p2
    chosen = jnp.where(lane == i1, 1.0, jnp.where(lane == i2, 1.0, 0.0))
    r = lax.broadcasted_iota(jnp.int32, (tm, tm), 0)
    c = lax.broadcasted_iota(jnp.int32, (tm, tm), 1)
    before = jnp.where(r > c, 1.0, 0.0).astype(BF16)
    rank = jnp.dot(before, chosen.astype(BF16), preferred_element_type=F32)
    rank1 = jnp.sum(jnp.where(lane == i1, rank, 0.0), axis=-1, keepdims=True)
    rank2 = jnp.sum(jnp.where(lane == i2, rank, 0.0), axis=-1, keepdims=True)
    meta = _lane_pick(lane, [i1.astype(F32), i2.astype(F32), p1 / den, p2 / den, rank1, rank2])
    counts = jnp.sum(chosen, axis=0, keepdims=True)
    return meta, counts


def _outproj_kernel(o_ref, w_ref, x_ref, mod_ref, nw_ref, wr_ref, br_ref, x1_ref, h2_ref, meta_ref, cnt_ref):
    y = jnp.dot(o_ref[...], w_ref[...], preferred_element_type=F32)
    x1 = x_ref[...] + mod_ref[2:3, :] * y
    x1_ref[...] = x1
    h2 = _norm_mod(x1, nw_ref[...], mod_ref, 3, 4)
    h2_ref[...] = h2.astype(BF16)
    meta, counts = _route(h2, wr_ref[...], br_ref[...])
    meta_ref[...] = meta
    cnt_ref[...] = jnp.broadcast_to(counts, cnt_ref.shape)


def _outproj_route(o, w, x2, mod4, layer, seq, tb, ffn_nw, wr_pad, br_pad):
    t, d = x2.shape
    per_b = seq // tb
    full = lambda a: pl.BlockSpec(a.shape, lambda i: (0,) * a.ndim)
    rows = lambda width: pl.BlockSpec((tb, width), lambda i: (i, 0))
    return pl.pallas_call(
        _outproj_kernel,
        grid=(t // tb,),
        in_specs=[rows(o.shape[1]), full(w), rows(d),
                  pl.BlockSpec((None, None, 6, d), lambda i: (layer, i // per_b, 0, 0)),
                  pl.BlockSpec((1, d), lambda i: (0, 0)), full(wr_pad), full(br_pad)],
        out_specs=[rows(d), rows(d), rows(LANES), pl.BlockSpec((None, SUBLANES, LANES), lambda i: (i, 0, 0))],
        out_shape=[jax.ShapeDtypeStruct((t, d), F32),
                   jax.ShapeDtypeStruct((t, d), BF16),
                   jax.ShapeDtypeStruct((t, LANES), F32),
                   jax.ShapeDtypeStruct((t // tb, SUBLANES, LANES), F32)],
        compiler_params=_cparams(("parallel",)),
        name=f"outproj_route_l{layer}",
    )(o, w, x2, mod4, ffn_nw.reshape(1, d), wr_pad, br_pad)


class _MoePlan:
    def __init__(self, t, tb):
        self.tb = tb
        self.n_tiles = t // tb
        self.local_rows = -(-(TOP_K * tb + N_EXPERTS * (ROW_GROUP - 1)) // PERM_BLOCK) * PERM_BLOCK
        self.groups = self.local_rows // ROW_GROUP
        self.ffn_tile = min(FFN_ROW_TILE, TOP_K * tb)
        worst_rows = TOP_K * t + self.n_tiles * N_EXPERTS * (ROW_GROUP - 1)
        self.ffn_tiles = -(-worst_rows // self.ffn_tile) + N_EXPERTS
        self.sorted_rows = self.ffn_tiles * self.ffn_tile


def _moe_tables(counts, plan):
    i32 = jnp.int32
    n_pad = (counts + (ROW_GROUP - 1)) // ROW_GROUP * ROW_GROUP
    ends = jnp.cumsum(n_pad, axis=1)
    loc = ends - n_pad
    n_groups = ends[:, -1] // ROW_GROUP
    n_e = jnp.sum(n_pad, axis=0)
    n_e_pad = -(-n_e // plan.ffn_tile) * plan.ffn_tile
    region_end = jnp.cumsum(n_e_pad)
    base = region_end - n_e_pad
    goff = base[None, :] + jnp.cumsum(n_pad, axis=0) - n_pad
    grow = jnp.arange(plan.groups, dtype=i32) * ROW_GROUP
    owner = jnp.minimum(jnp.sum(ends[:, None, :] <= grow[None, :, None], axis=-1), N_EXPERTS - 1)
    dest = jnp.take_along_axis(goff - loc, owner, axis=1) + grow[None, :]
    dest = jnp.where(jnp.arange(plan.groups)[None, :] < n_groups[:, None], dest, 0)
    n_used = region_end[-1] // plan.ffn_tile
    tile_row = jnp.minimum(jnp.arange(plan.ffn_tiles, dtype=i32), n_used - 1) * plan.ffn_tile
    tile_expert = jnp.minimum(jnp.sum(region_end[None, :] <= tile_row[:, None], axis=-1), N_EXPERTS - 1)
    return dict(loc=loc, dest=dest.reshape(-1).astype(i32), n_groups=n_groups.astype(i32),
                tail_start=(base + n_e).astype(i32), tail_groups=((n_e_pad - n_e) // ROW_GROUP).astype(i32),
                tile_expert=tile_expert.astype(i32), n_used=n_used.reshape(1).astype(i32))


def _group_copy(buf_ref, slot, g, hbm_ref, row, sem, to_hbm):
    aligned = lambda v: v if isinstance(v, int) else pl.multiple_of(v, ROW_GROUP)
    local = buf_ref.at[slot, pl.ds(aligned(g * ROW_GROUP), ROW_GROUP)]
    remote = hbm_ref.at[pl.ds(aligned(row), ROW_GROUP)]
    src, dst = (local, remote) if to_hbm else (remote, local)
    return pltpu.make_async_copy(src, dst, sem.at[slot])


def _dispatch_kernel(dest_ref, ng_ref, tstart_ref, tgroups_ref, nu_ref, h_ref, meta_ref, loc_ref,
                     sorted_ref, cmeta_ref, buf_ref, zero_ref, sem, zsem, *, plan):
    i = pl.program_id(0)
    last = pl.num_programs(0) - 1
    slot = i % 2
    tb, rows_l, groups = plan.tb, plan.local_rows, plan.groups

    def tail_copy(e, k):
        row = pl.multiple_of(tstart_ref[e] + k * ROW_GROUP, ROW_GROUP)
        return pltpu.make_async_copy(zero_ref.at[pl.ds(0, ROW_GROUP)], sorted_ref.at[pl.ds(row, ROW_GROUP)], zsem.at[0])

    def spare_copy(j):
        row = j * plan.ffn_tile if isinstance(j, int) else pl.multiple_of(j * plan.ffn_tile, plan.ffn_tile)
        return pltpu.make_async_copy(zero_ref, sorted_ref.at[pl.ds(row, plan.ffn_tile)], zsem.at[1])

    @pl.when(i == 0)
    def _():
        zero_ref[...] = jnp.zeros_like(zero_ref)
        for e in range(N_EXPERTS):
            lax.fori_loop(0, tgroups_ref[e], lambda k, c, e=e: (tail_copy(e, k).start(), c)[1], 0)
        lax.fori_loop(nu_ref[0], plan.ffn_tiles, lambda j, c: (spare_copy(j).start(), c)[1], 0)

    def wait_tile(tile, s):
        lax.fori_loop(0, ng_ref[tile],
                      lambda g, c: (_group_copy(buf_ref, s, 0, sorted_ref, 0, sem, True).wait(), c)[1], 0)

    @pl.when(i >= 2)
    def _():
        wait_tile(i - 2, slot)

    meta = meta_ref[...]
    lane = lax.broadcasted_iota(jnp.int32, meta.shape, 1)
    lanef = lane.astype(F32)
    loc_row = loc_ref[0:1, :]

    def local_row(e_col, rank_col):
        return jnp.sum(jnp.where(lanef == e_col, loc_row, 0.0), axis=-1, keepdims=True) + rank_col

    r1 = local_row(meta[:, 0:1], meta[:, 4:5])
    r2 = local_row(meta[:, 1:2], meta[:, 5:6])
    cmeta_ref[...] = _lane_pick(lane, [r1, r2, meta[:, 2:3], meta[:, 3:4]])
    r1h, r2h = jnp.floor(r1 * (1.0 / 256.0)), jnp.floor(r2 * (1.0 / 256.0))
    pieces = _lane_pick(lane, [r1h, r1 - 256.0 * r1h, r2h, r2 - 256.0 * r2h]).astype(BF16)
    pick = jnp.where(lax.broadcasted_iota(jnp.int32, (SUBLANES, LANES), 0)
                     == lax.broadcasted_iota(jnp.int32, (SUBLANES, LANES), 1), 1.0, 0.0).astype(BF16)
    rows_t = _dot_nt(pick, pieces)
    r1_row = rows_t[0:1, :] * 256.0 + rows_t[1:2, :]
    r2_row = rows_t[2:3, :] * 256.0 + rows_t[3:4, :]
    h = h_ref[...]
    blk = PERM_BLOCK
    for rb in range(rows_l // blk):
        rid = (lax.broadcasted_iota(jnp.int32, (blk, tb), 0) + rb * blk).astype(F32)
        perm = jnp.where(rid == r1_row, 1.0, jnp.where(rid == r2_row, 1.0, 0.0)).astype(BF16)
        buf_ref[slot, rb * blk:(rb + 1) * blk, :] = jnp.dot(perm, h, preferred_element_type=F32).astype(BF16)

    lax.fori_loop(0, ng_ref[i],
                  lambda g, c: (_group_copy(buf_ref, slot, g, sorted_ref, dest_ref[i * groups + g], sem, True).start(), c)[1],
                  0)

    @pl.when(i == last)
    def _():
        @pl.when(i >= 1)
        def _():
            wait_tile(i - 1, 1 - slot)
        wait_tile(i, slot)
        for e in range(N_EXPERTS):
            lax.fori_loop(0, tgroups_ref[e], lambda k, c, e=e: (tail_copy(e, 0).wait(), c)[1], 0)
        lax.fori_loop(nu_ref[0], plan.ffn_tiles, lambda j, c: (spare_copy(0).wait(), c)[1], 0)


def _dispatch(h2, meta, tabs, plan):
    t, d = h2.shape
    loc8 = jnp.broadcast_to(
        jnp.pad(tabs["loc"].astype(F32), ((0, 0), (0, LANES - N_EXPERTS)))[:, None, :], (plan.n_tiles, SUBLANES, LANES))
    grid_spec = pltpu.PrefetchScalarGridSpec(
        num_scalar_prefetch=5,
        grid=(plan.n_tiles,),
        in_specs=[pl.BlockSpec((plan.tb, d), lambda i, *_: (i, 0)),
                  pl.BlockSpec((plan.tb, LANES), lambda i, *_: (i, 0)),
                  pl.BlockSpec((None, SUBLANES, LANES), lambda i, *_: (i, 0, 0))],
        out_specs=[pl.BlockSpec(memory_space=pl.ANY),
                   pl.BlockSpec((plan.tb, LANES), lambda i, *_: (i, 0))],
        scratch_shapes=[pltpu.VMEM((2, plan.local_rows, d), BF16),
                        pltpu.VMEM((plan.ffn_tile, d), BF16),
                        pltpu.SemaphoreType.DMA((2,)),
                        pltpu.SemaphoreType.DMA((2,))])
    return pl.pallas_call(
        functools.partial(_dispatch_kernel, plan=plan),
        grid_spec=grid_spec,
        out_shape=[jax.ShapeDtypeStruct((plan.sorted_rows, d), BF16),
                   jax.ShapeDtypeStruct((t, LANES), F32)],
        compiler_params=_cparams(("arbitrary",)),
        name="moe_dispatch",
    )(tabs["dest"], tabs["n_groups"], tabs["tail_start"], tabs["tail_groups"], tabs["n_used"], h2, meta, loc8)


def _ffn_kernel(te_ref, nu_ref, x_ref, wg_ref, wu_ref, wd_ref, y_ref):
    used = pl.program_id(0) < nu_ref[0]

    @pl.when(used)
    def _():
        x = x_ref[...]
        g = jnp.dot(x, wg_ref[0], preferred_element_type=F32)
        u = jnp.dot(x, wu_ref[0], preferred_element_type=F32)
        y_ref[...] = jnp.dot((_silu(g) * u).astype(BF16), wd_ref[0], preferred_element_type=F32).astype(BF16)

    @pl.when(jnp.logical_not(used))
    def _():
        y_ref[...] = jnp.zeros_like(y_ref)


def _ffn(xs, wg, wu, wd, tabs, plan):
    d, de = wg.shape[1], wg.shape[2]
    tm = plan.ffn_tile
    row_blk = lambda j, te, nu: (jnp.minimum(j, nu[0] - 1), 0)
    grid_spec = pltpu.PrefetchScalarGridSpec(
        num_scalar_prefetch=2,
        grid=(plan.ffn_tiles,),
        in_specs=[pl.BlockSpec((tm, d), row_blk),
                  pl.BlockSpec((1, d, de), lambda j, te, nu: (te[j], 0, 0)),
                  pl.BlockSpec((1, d, de), lambda j, te, nu: (te[j], 0, 0)),
                  pl.BlockSpec((1, de, d), lambda j, te, nu: (te[j], 0, 0))],
        out_specs=pl.BlockSpec((tm, d), lambda j, te, nu: (j, 0)))
    return pl.pallas_call(
        _ffn_kernel,
        grid_spec=grid_spec,
        out_shape=jax.ShapeDtypeStruct(xs.shape, BF16),
        compiler_params=_cparams(("arbitrary",)),
        name="moe_ffn",
    )(tabs["tile_expert"], tabs["n_used"], xs, wg, wu, wd)


def _combine_kernel(dest_ref, ng_ref, ys_ref, cmeta_ref, x_ref, mod_ref, fw_ref, o_ref, buf_ref, sem,
                    *, plan, final):
    i = pl.program_id(0)
    n_tiles = pl.num_programs(0)
    slot = i % 2
    groups = plan.groups

    def fetch(tile, s):
        lax.fori_loop(0, ng_ref[tile],
                      lambda g, c: (_group_copy(buf_ref, s, g, ys_ref, dest_ref[tile * groups + g], sem, False).start(), c)[1],
                      0)

    @pl.when(i == 0)
    def _():
        buf_ref[...] = jnp.zeros_like(buf_ref)
        fetch(0, 0)

    @pl.when(i + 1 < n_tiles)
    def _():
        fetch(i + 1, 1 - slot)

    lax.fori_loop(0, ng_ref[i],
                  lambda g, c: (_group_copy(buf_ref, slot, 0, ys_ref, 0, sem, False).wait(), c)[1], 0)

    cm = cmeta_ref[...]
    rid = lax.broadcasted_iota(jnp.int32, (plan.tb, plan.local_rows), 1).astype(F32)
    comb = (jnp.where(rid == cm[:, 0:1], cm[:, 2:3], 0.0) + jnp.where(rid == cm[:, 1:2], cm[:, 3:4], 0.0)).astype(BF16)
    moe = jnp.dot(comb, buf_ref[slot], preferred_element_type=F32)
    xn = x_ref[...] + mod_ref[5:6, :] * moe
    if final:
        ms = jnp.mean(xn * xn, axis=-1, keepdims=True)
        xn = xn * lax.rsqrt(ms + EPS) * fw_ref[...]
    o_ref[...] = xn


def _combine(ys, cmeta, x1, mod4, layer, seq, tabs, plan, final_w, final):
    t, d = x1.shape
    per_b = seq // plan.tb
    grid_spec = pltpu.PrefetchScalarGridSpec(
        num_scalar_prefetch=2,
        grid=(plan.n_tiles,),
        in_specs=[pl.BlockSpec(memory_space=pl.ANY),
                  pl.BlockSpec((plan.tb, LANES), lambda i, *_: (i, 0)),
                  pl.BlockSpec((plan.tb, d), lambda i, *_: (i, 0)),
                  pl.BlockSpec((None, None, 6, d), lambda i, *_: (layer, i // per_b, 0, 0)),
                  pl.BlockSpec((1, d), lambda i, *_: (0, 0))],
        out_specs=pl.BlockSpec((plan.tb, d), lambda i, *_: (i, 0)),
        scratch_shapes=[pltpu.VMEM((2, plan.local_rows, d), BF16),
                        pltpu.SemaphoreType.DMA((2,))])
    return pl.pallas_call(
        functools.partial(_combine_kernel, plan=plan, final=final),
        grid_spec=grid_spec,
        out_shape=jax.ShapeDtypeStruct((t, d), F32),
        compiler_params=_cparams(("arbitrary",)),
        name="moe_combine",
    )(tabs["dest"], tabs["n_groups"], ys, cmeta, x1, mod4, final_w.reshape(1, d))


def kernel(x, c, w_ada, b_ada, norm_mix_w, norm_ffn_w, w_in_even, hg_lb_logits, hg_norm_w, ret_norm_w,
           w_out_even, w_in_odd, w_gla_a2, b_gla_a2, gla_norm_w, w_out_odd, w_router, b_router,
           w_gate, w_up, w_down, final_norm_w):
    bsz, seq, d = x.shape
    depth = w_ada.shape[0]
    x2 = x.reshape(bsz * seq, d)
    mod4 = _adaln_mod(c, w_ada, b_ada).reshape(depth, bsz, 6, d)
    wr_pad = jnp.pad(w_router, ((0, 0), (0, LANES - N_EXPERTS)))
    br_pad = jnp.pad(b_router, (0, LANES - N_EXPERTS)).reshape(1, LANES)
    plan = _MoePlan(bsz * seq, min(MOE_TOKEN_TILE, seq))
    for l in range(depth):
        j = l // 2
        if l % 2 == 0:
            proj, hf = _inproj(x2, mod4, l, seq, norm_mix_w[l], w_in_even[j].astype(BF16),
                               _inproj_even_kernel, w_in_even.shape[-1], [], [])
            o = _even_mixers(proj, hf, hg_lb_logits, hg_norm_w[j], ret_norm_w[j], bsz, seq, l)
            w_out = w_out_even[j].astype(BF16)
        else:
            n_main = w_in_odd.shape[-1] - GLA_RANK
            wi = w_in_odd[j]
            wa = jnp.pad(wi[:, n_main:], ((0, 0), (0, LANES - GLA_RANK))).astype(BF16)
            wa2 = jnp.pad(w_gla_a2[j], ((0, LANES - GLA_RANK), (0, 0))).astype(BF16)
            full = lambda a: pl.BlockSpec(a.shape, lambda i: (0,) * a.ndim)
            ba2 = b_gla_a2[j].reshape(1, -1)
            proj, log_a = _inproj(x2, mod4, l, seq, norm_mix_w[l], wi[:, :n_main].astype(BF16),
                                  _inproj_odd_kernel, n_main, [full(wa), full(wa2), full(ba2)], [wa, wa2, ba2])
            o = _odd_mixer(proj, log_a, gla_norm_w[j], bsz, seq, l)
            w_out = w_out_odd[j].astype(BF16)
        x1, h2, meta, counts = _outproj_route(o, w_out, x2, mod4, l, seq, plan.tb, norm_ffn_w[l], wr_pad, br_pad)
        tabs = _moe_tables(counts[:, 0, :N_EXPERTS].astype(jnp.int32), plan)
        xs, cmeta = _dispatch(h2, meta, tabs, plan)
        ys = _ffn(xs, w_gate[l].astype(BF16), w_up[l].astype(BF16), w_down[l].astype(BF16), tabs, plan)
        x2 = _combine(ys, cmeta, x1, mod4, l, seq, tabs, plan, final_norm_w, final=(l == depth - 1))
    return x2.reshape(bsz, seq, d)
```

```python
import functools

import jax
import jax.numpy as jnp
from jax import lax
from jax.experimental import pallas as pl
from jax.experimental.pallas import tpu as pltpu

F32 = jnp.float32
BF16 = jnp.bfloat16
HIGHEST = lax.Precision.HIGHEST

EPS = 1e-6
CHUNK = 64
HEAD_DIM = 128
N_HEADS = 4
N_EXPERTS = 16
N_GROUPS = 4
EXPERTS_PER_GROUP = N_EXPERTS // N_GROUPS
GLA_RANK = 16
GLA_GATE_NORM = 16.0
ROPE_BASE = 10000.0
LANES = 128
VMEM_LIMIT = 52 * 1024 * 1024
SUBLANES = 8
MIX_SEQ_TILE = 512
TOP_K = 2
ROW_GROUP = 16
MOE_TOKEN_TILE = 512
FFN_ROW_TILE = 512
PERM_BLOCK = 256
ROUTE_BLOCKS = 2


def _cparams(sem):
    return pltpu.CompilerParams(dimension_semantics=sem, vmem_limit_bytes=VMEM_LIMIT)


def _sigmoid(z):
    return 0.5 * jnp.tanh(0.5 * z) + 0.5


def _silu(z):
    return z * _sigmoid(z)


def _mod_kernel(c_ref, w_ref, b_ref, o_ref):
    c = c_ref[...]
    o_ref[0] = jnp.dot(_silu(c), w_ref[0], preferred_element_type=F32, precision=HIGHEST) + b_ref[0]


def _adaln_mod(c, w_ada, b_ada):
    depth, d, n = w_ada.shape
    bsz = c.shape[0]
    tn = 1536
    return pl.pallas_call(
        _mod_kernel,
        grid=(depth, n // tn),
        in_specs=[pl.BlockSpec((bsz, d), lambda l, j: (0, 0)),
                  pl.BlockSpec((1, d, tn), lambda l, j: (l, 0, j)),
                  pl.BlockSpec((1, 1, tn), lambda l, j: (l, 0, j))],
        out_specs=pl.BlockSpec((1, bsz, tn), lambda l, j: (l, 0, j)),
        out_shape=jax.ShapeDtypeStruct((depth, bsz, n), F32),
        compiler_params=_cparams(("arbitrary", "arbitrary")),
        name="adaln_mod",
    )(c, w_ada, b_ada.reshape(depth, 1, n))


def _norm_mod(x, nw, mod_ref, shift_row, scale_row):
    ms = jnp.mean(x * x, axis=-1, keepdims=True)
    return x * lax.rsqrt(ms + EPS) * (nw * (1.0 + mod_ref[scale_row:scale_row + 1, :])) + mod_ref[shift_row:shift_row + 1, :]


def _inproj_even_kernel(x_ref, mod_ref, nw_ref, w_ref, o_ref, f_ref):
    h = _norm_mod(x_ref[...], nw_ref[...], mod_ref, 0, 1)
    res = jnp.dot(h.astype(BF16), w_ref[...], preferred_element_type=F32)
    o_ref[...] = res.astype(BF16)
    f_ref[...] = res[:, 512:1024]


def _log_sigmoid(z):
    return jnp.minimum(z, 0.0) - jnp.log1p(jnp.exp(-jnp.abs(z)))


def _inproj_odd_kernel(x_ref, mod_ref, nw_ref, w_ref, wa_ref, wa2_ref, ba2_ref, o_ref, la_ref):
    h = _norm_mod(x_ref[...], nw_ref[...], mod_ref, 0, 1).astype(BF16)
    o_ref[...] = jnp.dot(h, w_ref[...], preferred_element_type=F32).astype(BF16)
    ga = jnp.dot(h, wa_ref[...], preferred_element_type=F32)
    z = jnp.dot(ga.astype(BF16), wa2_ref[...], preferred_element_type=F32) + ba2_ref[...]
    la_ref[...] = _log_sigmoid(z) * (1.0 / GLA_GATE_NORM)


def _inproj(x2, mod4, layer, seq, nw, weights, kernel_fn, n_main, extra_specs, extra_args):
    t, d = x2.shape
    tm = min(256, seq)
    per_b = seq // tm
    return pl.pallas_call(
        kernel_fn,
        grid=(t // tm,),
        in_specs=[pl.BlockSpec((tm, d), lambda i: (i, 0)),
                  pl.BlockSpec((None, None, 6, d), lambda i: (layer, i // per_b, 0, 0)),
                  pl.BlockSpec((1, d), lambda i: (0, 0)),
                  pl.BlockSpec((d, n_main), lambda i: (0, 0))] + extra_specs,
        out_specs=[pl.BlockSpec((tm, n_main), lambda i: (i, 0)),
                   pl.BlockSpec((tm, 512), lambda i: (i, 0))],
        out_shape=[jax.ShapeDtypeStruct((t, n_main), BF16),
                   jax.ShapeDtypeStruct((t, 512), F32)],
        compiler_params=_cparams(("parallel",)),
        name=f"inproj_l{layer}",
    )(x2, mod4, nw.reshape(1, d), weights, *extra_args)


def _causal_mask():
    r = lax.broadcasted_iota(jnp.int32, (CHUNK, CHUNK), 0)
    c = lax.broadcasted_iota(jnp.int32, (CHUNK, CHUNK), 1)
    return r >= c


def _dot_nt(a, b):
    return lax.dot_general(a, b, (((1,), (1,)), ((), ())), preferred_element_type=F32)


def _dot_tn(a, b):
    return lax.dot_general(a, b, (((0,), (0,)), ((), ())), preferred_element_type=F32)


def _head_out(o, g, nw, n_norm):
    ms = jnp.sum(o * o, axis=-1, keepdims=True) * (1.0 / n_norm)
    return o * lax.rsqrt(ms + EPS) * nw * _silu(g)


def _cumsum_rows(lg):
    tril = jnp.where(_causal_mask(), 1.0, 0.0).astype(BF16)
    hi = lg.astype(BF16)
    lo = (lg - hi.astype(F32)).astype(BF16)
    return jnp.dot(tril, hi, preferred_element_type=F32) + jnp.dot(tril, lo, preferred_element_type=F32)


def _decay_heads(q, k, v, b, st_ref, st_base, dv):
    b_mid = b[CHUNK // 2 - 1:CHUNK // 2, :]
    b_last = b[CHUNK - 1:CHUNK, :]
    qe = q * jnp.exp(b - b_mid)
    ke = k * jnp.exp(b_mid - b)
    qs = qe.astype(BF16)
    ks = ke.astype(BF16)
    qi = (qe * jnp.exp(b_mid)).astype(BF16)
    kd = (ke * jnp.exp(b_last - b_mid)).astype(BF16)
    dec = jnp.exp(b_last)
    mask = _causal_mask()
    outs = []
    for h in range(N_HEADS):
        kl = slice(h * HEAD_DIM, (h + 1) * HEAD_DIM)
        vh = v[:, h * dv:(h + 1) * dv]
        scores = jnp.where(mask, _dot_nt(qs[:, kl], ks[:, kl]), 0.0)
        o = jnp.dot(scores.astype(BF16), vh, preferred_element_type=F32)
        st = st_ref[st_base + h]
        o = o + _dot_nt(qi[:, kl], st.astype(BF16))
        st_ref[st_base + h] = dec[:, kl] * st + _dot_tn(vh, kd[:, kl])
        outs.append(o)
    return outs


def _even_mixer_kernel(proj_ref, hf_ref, lbl_ref, hgnw_ref, retnw_ref, cos_ref, sin_ref, o_ref, st_ref,
                       *, layer, n_chunks):
    hd, w = HEAD_DIM, N_HEADS * HEAD_DIM

    @pl.when(pl.program_id(1) == 0)
    def _():
        st_ref[...] = jnp.zeros_like(st_ref)

    lbl = lbl_ref[...]
    e = jnp.exp(lbl - jnp.max(lbl, axis=0, keepdims=True))
    p = e / jnp.sum(e, axis=0, keepdims=True)
    lb = jnp.sum(p[:layer + 1], axis=0, keepdims=True)
    hgnw = hgnw_ref[...]
    retnw = retnw_ref[...]
    headf = (lax.broadcasted_iota(jnp.int32, (1, w), 1) // hd).astype(F32)
    lg = jnp.log1p(-jnp.exp2(-5.0 - headf))
    pos = lax.broadcasted_iota(jnp.int32, (CHUNK, w), 0).astype(F32)
    e_q = jnp.exp((pos + 1.0) * lg)
    e_k = jnp.exp((CHUNK - 1.0 - pos) * lg)
    e_c = jnp.exp(CHUNK * lg)
    r = lax.broadcasted_iota(jnp.int32, (CHUNK, CHUNK), 0)
    cc = lax.broadcasted_iota(jnp.int32, (CHUNK, CHUNK), 1)
    dt = (r - cc).astype(F32)
    dmasks = [jnp.where(r >= cc, jnp.exp(dt * lg[:, h * hd:h * hd + CHUNK]), 0.0) for h in range(N_HEADS)]

    def rot(xx, cs, sn):
        return xx * cs + pltpu.roll(xx, hd // 2, 1) * sn

    def body(c, carry):
        sl = pl.ds(pl.multiple_of(c * CHUNK, CHUNK), CHUNK)
        f = lb + (1.0 - lb) * _sigmoid(hf_ref[sl, :])
        b = _cumsum_rows(jnp.log(f))
        q = _sigmoid(proj_ref[sl, 0:w].astype(F32))
        outs = _decay_heads(q, 1.0 - f, proj_ref[sl, 2 * w:3 * w], b, st_ref, 0, hd)
        for h, o in enumerate(outs):
            cl = slice(h * hd, (h + 1) * hd)
            g = proj_ref[sl, 3 * w + h * hd:3 * w + (h + 1) * hd].astype(F32)
            o_ref[sl, cl] = _head_out(o, g, hgnw[:, cl], hd).astype(BF16)
        cs = cos_ref[sl, :]
        sn = sin_ref[sl, :]
        for h in range(N_HEADS):
            cl = slice(h * hd, (h + 1) * hd)
            qh = rot(proj_ref[sl, 4 * w + h * hd:4 * w + (h + 1) * hd].astype(F32), cs, sn)
            kh = rot(proj_ref[sl, 5 * w + h * hd:5 * w + (h + 1) * hd].astype(F32), cs, sn) * (hd ** -0.5)
            vh = proj_ref[sl, 6 * w + h * hd:6 * w + (h + 1) * hd]
            scores = _dot_nt(qh.astype(BF16), kh.astype(BF16)) * dmasks[h]
            o = jnp.dot(scores.astype(BF16), vh, preferred_element_type=F32)
            st = st_ref[N_HEADS + h]
            o = o + _dot_nt((qh * e_q[:, cl]).astype(BF16), st.astype(BF16))
            st_ref[N_HEADS + h] = e_c[:, cl] * st + _dot_tn(vh, (kh * e_k[:, cl]).astype(BF16))
            g = proj_ref[sl, 7 * w + h * hd:7 * w + (h + 1) * hd].astype(F32)
            o_ref[sl, w + h * hd:w + (h + 1) * hd] = _head_out(o, g, retnw[:, cl], hd).astype(BF16)
        return carry

    lax.fori_loop(0, n_chunks, body, 0, unroll=True)


def _gla_mixer_kernel(proj_ref, la_ref, nw_ref, o_ref, st_ref, *, n_chunks):
    hd, w = HEAD_DIM, N_HEADS * HEAD_DIM
    dv = 2 * hd

    @pl.when(pl.program_id(1) == 0)
    def _():
        st_ref[...] = jnp.zeros_like(st_ref)

    nw = nw_ref[...]

    def body(c, carry):
        sl = pl.ds(pl.multiple_of(c * CHUNK, CHUNK), CHUNK)
        b = _cumsum_rows(la_ref[sl, :])
        q = proj_ref[sl, 0:w].astype(F32) * (hd ** -0.5)
        k = proj_ref[sl, w:2 * w].astype(F32)
        outs = _decay_heads(q, k, proj_ref[sl, 2 * w:2 * w + N_HEADS * dv], b, st_ref, 0, dv)
        for h, o in enumerate(outs):
            cl = slice(h * dv, (h + 1) * dv)
            g = proj_ref[sl, 4 * w + h * dv:4 * w + (h + 1) * dv].astype(F32)
            o_ref[sl, cl] = _head_out(o, g, nw[:, cl], dv).astype(BF16)
        return carry

    lax.fori_loop(0, n_chunks, body, 0, unroll=True)


def _mixer_call(kernel_fn, bsz, seq, ts, in_specs, n_states, dv, name, args):
    d_out = N_HEADS * 2 * HEAD_DIM
    return pl.pallas_call(
        kernel_fn,
        grid=(bsz, seq // ts),
        in_specs=in_specs,
        out_specs=pl.BlockSpec((ts, d_out), lambda b, s: (b * (seq // ts) + s, 0)),
        out_shape=jax.ShapeDtypeStruct((bsz * seq, d_out), BF16),
        scratch_shapes=[pltpu.VMEM((n_states, dv, HEAD_DIM), F32)],
        compiler_params=_cparams(("parallel", "arbitrary")),
        name=name,
    )(*args)


def _even_mixers(proj, hf, lb_logits, hg_nw, ret_nw, bsz, seq, layer):
    ts = min(MIX_SEQ_TILE, seq)
    per_b = seq // ts
    hd = HEAD_DIM
    rows = lambda width: pl.BlockSpec((ts, width), lambda b, s: (b * per_b + s, 0))
    full = lambda a: pl.BlockSpec(a.shape, lambda b, s: (0,) * a.ndim)
    inv = ROPE_BASE ** (-jnp.arange(0, hd, 2, dtype=F32) / hd)
    ang = jnp.arange(seq, dtype=F32)[:, None] * inv[None, :]
    cos2 = jnp.concatenate([jnp.cos(ang), jnp.cos(ang)], axis=-1)
    sin2 = jnp.concatenate([-jnp.sin(ang), jnp.sin(ang)], axis=-1)
    tab = pl.BlockSpec((ts, hd), lambda b, s: (s, 0))
    hg_nw, ret_nw = hg_nw.reshape(1, -1), ret_nw.reshape(1, -1)
    return _mixer_call(
        functools.partial(_even_mixer_kernel, layer=layer, n_chunks=ts // CHUNK), bsz, seq, ts,
        [rows(proj.shape[1]), rows(hf.shape[1]), full(lb_logits), full(hg_nw), full(ret_nw), tab, tab],
        2 * N_HEADS, hd, f"even_mixer_l{layer}",
        (proj, hf, lb_logits, hg_nw, ret_nw, cos2, sin2))


def _odd_mixer(proj, log_a, gla_nw, bsz, seq, layer):
    ts = min(MIX_SEQ_TILE, seq)
    per_b = seq // ts
    rows = lambda width: pl.BlockSpec((ts, width), lambda b, s: (b * per_b + s, 0))
    gla_nw = gla_nw.reshape(1, -1)
    return _mixer_call(
        functools.partial(_gla_mixer_kernel, n_chunks=ts // CHUNK), bsz, seq, ts,
        [rows(proj.shape[1]), rows(log_a.shape[1]), pl.BlockSpec(gla_nw.shape, lambda b, s: (0, 0))],
        N_HEADS, 2 * HEAD_DIM, f"gla_mixer_l{layer}",
        (proj, log_a, gla_nw))


def _lane_pick(lane, cols):
    out = jnp.zeros(lane.shape, F32)
    for j, col in reversed(list(enumerate(cols))):
        out = jnp.where(lane == j, col, out)
    return out


def _route(h2, wr, br, prior):
    neg = -jnp.inf
    logits = jnp.dot(h2, wr, preferred_element_type=F32)
    tm = logits.shape[0]
    lane = lax.broadcasted_iota(jnp.int32, logits.shape, 1)
    valid = lane < N_EXPERTS
    lm = jnp.where(valid, logits, neg)
    ex = jnp.where(valid, jnp.exp(lm - jnp.max(lm, axis=-1, keepdims=True)), 0.0)
    probs = ex / jnp.sum(ex, axis=-1, keepdims=True)
    sel = jnp.where(valid, probs + br, neg)
    grp = lane // EXPERTS_PER_GROUP
    gs = [jnp.max(jnp.where(grp == g, sel, neg), axis=-1, keepdims=True) for g in range(N_GROUPS)]
    best = functools.reduce(jnp.maximum, gs)
    gidx = jnp.full(best.shape, N_GROUPS - 1, jnp.int32)
    for g in range(N_GROUPS - 2, -1, -1):
        gidx = jnp.where(gs[g] == best, g, gidx)
    masked = jnp.where(grp == gidx, sel, neg)
    m1 = jnp.max(masked, axis=-1, keepdims=True)
    i1 = jnp.min(jnp.where(masked == m1, lane, LANES), axis=-1, keepdims=True)
    masked2 = jnp.where(lane == i1, neg, masked)
    m2 = jnp.max(masked2, axis=-1, keepdims=True)
    i2 = jnp.min(jnp.where(masked2 == m2, lane, LANES), axis=-1, keepdims=True)
    p1 = jnp.sum(jnp.where(lane == i1, probs, 0.0), axis=-1, keepdims=True)
    p2 = jnp.sum(jnp.where(lane == i2, probs, 0.0), axis=-1, keepdims=True)
    den = p1 + p2
    chosen = jnp.where(lane == i1, 1.0, jnp.where(lane == i2, 1.0, 0.0))
    r = lax.broadcasted_iota(jnp.int32, (tm, tm), 0)
    c = lax.broadcasted_iota(jnp.int32, (tm, tm), 1)
    before = jnp.where(r > c, 1.0, 0.0).astype(BF16)
    rank = jnp.dot(before, chosen.astype(BF16), preferred_element_type=F32) + prior
    rank1 = jnp.sum(jnp.where(lane == i1, rank, 0.0), axis=-1, keepdims=True)
    rank2 = jnp.sum(jnp.where(lane == i2, rank, 0.0), axis=-1, keepdims=True)
    meta = _lane_pick(lane, [i1.astype(F32), i2.astype(F32), p1 / den, p2 / den, rank1, rank2])
    return meta, prior + jnp.sum(chosen, axis=0, keepdims=True)


def _outproj_kernel(o_ref, w_ref, x_ref, mod_ref, nw_ref, wr_ref, br_ref, x1_ref, h2_ref, meta_ref, cnt_ref):
    tb = x_ref.shape[0]
    blk = tb // ROUTE_BLOCKS
    counts = jnp.zeros((1, LANES), F32)
    for s in range(ROUTE_BLOCKS):
        rs = slice(s * blk, (s + 1) * blk)
        y = jnp.dot(o_ref[rs, :], w_ref[...], preferred_element_type=F32)
        x1 = x_ref[rs, :] + mod_ref[2:3, :] * y
        x1_ref[rs, :] = x1
        h2 = _norm_mod(x1, nw_ref[...], mod_ref, 3, 4).astype(BF16)
        h2_ref[rs, :] = h2
        meta, counts = _route(h2, wr_ref[...], br_ref[...], counts)
        meta_ref[rs, :] = meta
    cnt_ref[...] = jnp.broadcast_to(counts, cnt_ref.shape)


def _outproj_route(o, w, x2, mod4, layer, seq, tb, ffn_nw, wr_pad, br_pad):
    t, d = x2.shape
    per_b = seq // tb
    full = lambda a: pl.BlockSpec(a.shape, lambda i: (0,) * a.ndim)
    rows = lambda width: pl.BlockSpec((tb, width), lambda i: (i, 0))
    return pl.pallas_call(
        _outproj_kernel,
        grid=(t // tb,),
        in_specs=[rows(o.shape[1]), full(w), rows(d),
                  pl.BlockSpec((None, None, 6, d), lambda i: (layer, i // per_b, 0, 0)),
                  pl.BlockSpec((1, d), lambda i: (0, 0)), full(wr_pad), full(br_pad)],
        out_specs=[rows(d), rows(d), rows(LANES), pl.BlockSpec((None, SUBLANES, LANES), lambda i: (i, 0, 0))],
        out_shape=[jax.ShapeDtypeStruct((t, d), F32),
                   jax.ShapeDtypeStruct((t, d), BF16),
                   jax.ShapeDtypeStruct((t, LANES), F32),
                   jax.ShapeDtypeStruct((t // tb, SUBLANES, LANES), F32)],
        compiler_params=_cparams(("parallel",)),
        name=f"outproj_route_l{layer}",
    )(o, w, x2, mod4, ffn_nw.reshape(1, d), wr_pad, br_pad)


class _MoePlan:
    def __init__(self, t, tb):
        self.tb = tb
        self.n_tiles = t // tb
        self.local_rows = -(-(TOP_K * tb + N_EXPERTS * (ROW_GROUP - 1)) // PERM_BLOCK) * PERM_BLOCK
        self.groups = self.local_rows // ROW_GROUP
        self.ffn_tile = min(FFN_ROW_TILE, TOP_K * tb)
        worst_rows = TOP_K * t + self.n_tiles * N_EXPERTS * (ROW_GROUP - 1)
        self.ffn_tiles = -(-worst_rows // self.ffn_tile) + N_EXPERTS
        self.sorted_rows = self.ffn_tiles * self.ffn_tile


def _moe_tables(counts, plan):
    i32 = jnp.int32
    n_pad = (counts + (ROW_GROUP - 1)) // ROW_GROUP * ROW_GROUP
    ends = jnp.cumsum(n_pad, axis=1)
    loc = ends - n_pad
    n_groups = ends[:, -1] // ROW_GROUP
    n_e = jnp.sum(n_pad, axis=0)
    n_e_pad = -(-n_e // plan.ffn_tile) * plan.ffn_tile
    region_end = jnp.cumsum(n_e_pad)
    base = region_end - n_e_pad
    goff = base[None, :] + jnp.cumsum(n_pad, axis=0) - n_pad
    grow = jnp.arange(plan.groups, dtype=i32) * ROW_GROUP
    owner = jnp.minimum(jnp.sum(ends[:, None, :] <= grow[None, :, None], axis=-1), N_EXPERTS - 1)
    shift = jnp.sum(jnp.where(owner[:, :, None] == jnp.arange(N_EXPERTS)[None, None, :],
                              (goff - loc)[:, None, :], 0), axis=-1)
    dest = shift + grow[None, :]
    dest = jnp.where(jnp.arange(plan.groups)[None, :] < n_groups[:, None], dest, 0)
    n_used = region_end[-1] // plan.ffn_tile
    tile_row = jnp.minimum(jnp.arange(plan.ffn_tiles, dtype=i32), n_used - 1) * plan.ffn_tile
    tile_expert = jnp.minimum(jnp.sum(region_end[None, :] <= tile_row[:, None], axis=-1), N_EXPERTS - 1)
    return dict(loc=loc, dest=dest.reshape(-1).astype(i32), n_groups=n_groups.astype(i32),
                tail_start=(base + n_e).astype(i32), tail_groups=((n_e_pad - n_e) // ROW_GROUP).astype(i32),
                tile_expert=tile_expert.astype(i32), n_used=n_used.reshape(1).astype(i32))


def _group_copy(buf_ref, slot, g, hbm_ref, row, sem, to_hbm):
    aligned = lambda v: v if isinstance(v, int) else pl.multiple_of(v, ROW_GROUP)
    local = buf_ref.at[slot, pl.ds(aligned(g * ROW_GROUP), ROW_GROUP)]
    remote = hbm_ref.at[pl.ds(aligned(row), ROW_GROUP)]
    src, dst = (local, remote) if to_hbm else (remote, local)
    return pltpu.make_async_copy(src, dst, sem.at[slot])


def _dispatch_kernel(dest_ref, ng_ref, tstart_ref, tgroups_ref, nu_ref, h_ref, meta_ref, loc_ref,
                     sorted_ref, cmeta_ref, buf_ref, zero_ref, sem, zsem, *, plan):
    i = pl.program_id(0)
    last = pl.num_programs(0) - 1
    slot = i % 2
    tb, rows_l, groups = plan.tb, plan.local_rows, plan.groups

    def tail_copy(e, k):
        row = pl.multiple_of(tstart_ref[e] + k * ROW_GROUP, ROW_GROUP)
        return pltpu.make_async_copy(zero_ref.at[pl.ds(0, ROW_GROUP)], sorted_ref.at[pl.ds(row, ROW_GROUP)], zsem.at[0])

    def spare_copy(j):
        row = j * plan.ffn_tile if isinstance(j, int) else pl.multiple_of(j * plan.ffn_tile, plan.ffn_tile)
        return pltpu.make_async_copy(zero_ref, sorted_ref.at[pl.ds(row, plan.ffn_tile)], zsem.at[1])

    @pl.when(i == 0)
    def _():
        zero_ref[...] = jnp.zeros_like(zero_ref)
        for e in range(N_EXPERTS):
            lax.fori_loop(0, tgroups_ref[e], lambda k, c, e=e: (tail_copy(e, k).start(), c)[1], 0)
        lax.fori_loop(nu_ref[0], plan.ffn_tiles, lambda j, c: (spare_copy(j).start(), c)[1], 0)

    def wait_tile(tile, s):
        lax.fori_loop(0, ng_ref[tile],
                      lambda g, c: (_group_copy(buf_ref, s, 0, sorted_ref, 0, sem, True).wait(), c)[1], 0)

    @pl.when(i >= 2)
    def _():
        wait_tile(i - 2, slot)

    meta = meta_ref[...]
    lane = lax.broadcasted_iota(jnp.int32, meta.shape, 1)
    lanef = lane.astype(F32)
    loc_row = loc_ref[0:1, :]

    def local_row(e_col, rank_col):
        return jnp.sum(jnp.where(lanef == e_col, loc_row, 0.0), axis=-1, keepdims=True) + rank_col

    r1 = local_row(meta[:, 0:1], meta[:, 4:5])
    r2 = local_row(meta[:, 1:2], meta[:, 5:6])
    cmeta_ref[...] = _lane_pick(lane, [r1, r2, meta[:, 2:3], meta[:, 3:4]])
    r1h, r2h = jnp.floor(r1 * (1.0 / 256.0)), jnp.floor(r2 * (1.0 / 256.0))
    pieces = _lane_pick(lane, [r1h, r1 - 256.0 * r1h, r2h, r2 - 256.0 * r2h]).astype(BF16)
    pick = jnp.where(lax.broadcasted_iota(jnp.int32, (SUBLANES, LANES), 0)
                     == lax.broadcasted_iota(jnp.int32, (SUBLANES, LANES), 1), 1.0, 0.0).astype(BF16)
    rows_t = _dot_nt(pick, pieces)
    r1_row = rows_t[0:1, :] * 256.0 + rows_t[1:2, :]
    r2_row = rows_t[2:3, :] * 256.0 + rows_t[3:4, :]
    h = h_ref[...]
    blk = PERM_BLOCK
    for rb in range(rows_l // blk):
        rid = (lax.broadcasted_iota(jnp.int32, (blk, tb), 0) + rb * blk).astype(F32)
        perm = jnp.where(rid == r1_row, 1.0, jnp.where(rid == r2_row, 1.0, 0.0)).astype(BF16)
        buf_ref[slot, rb * blk:(rb + 1) * blk, :] = jnp.dot(perm, h, preferred_element_type=F32).astype(BF16)

    lax.fori_loop(0, ng_ref[i],
                  lambda g, c: (_group_copy(buf_ref, slot, g, sorted_ref, dest_ref[i * groups + g], sem, True).start(), c)[1],
                  0)

    @pl.when(i == last)
    def _():
        @pl.when(i >= 1)
        def _():
            wait_tile(i - 1, 1 - slot)
        wait_tile(i, slot)
        for e in range(N_EXPERTS):
            lax.fori_loop(0, tgroups_ref[e], lambda k, c, e=e: (tail_copy(e, 0).wait(), c)[1], 0)
        lax.fori_loop(nu_ref[0], plan.ffn_tiles, lambda j, c: (spare_copy(0).wait(), c)[1], 0)


def _dispatch(h2, meta, tabs, plan):
    t, d = h2.shape
    loc8 = jnp.broadcast_to(
        jnp.pad(tabs["loc"].astype(F32), ((0, 0), (0, LANES - N_EXPERTS)))[:, None, :], (plan.n_tiles, SUBLANES, LANES))
    grid_spec = pltpu.PrefetchScalarGridSpec(
        num_scalar_prefetch=5,
        grid=(plan.n_tiles,),
        in_specs=[pl.BlockSpec((plan.tb, d), lambda i, *_: (i, 0)),
                  pl.BlockSpec((plan.tb, LANES), lambda i, *_: (i, 0)),
                  pl.BlockSpec((None, SUBLANES, LANES), lambda i, *_: (i, 0, 0))],
        out_specs=[pl.BlockSpec(memory_space=pl.ANY),
                   pl.BlockSpec((plan.tb, LANES), lambda i, *_: (i, 0))],
        scratch_shapes=[pltpu.VMEM((2, plan.local_rows, d), BF16),
                        pltpu.VMEM((plan.ffn_tile, d), BF16),
                        pltpu.SemaphoreType.DMA((2,)),
                        pltpu.SemaphoreType.DMA((2,))])
    return pl.pallas_call(
        functools.partial(_dispatch_kernel, plan=plan),
        grid_spec=grid_spec,
        out_shape=[jax.ShapeDtypeStruct((plan.sorted_rows, d), BF16),
                   jax.ShapeDtypeStruct((t, LANES), F32)],
        compiler_params=_cparams(("arbitrary",)),
        name="moe_dispatch",
    )(tabs["dest"], tabs["n_groups"], tabs["tail_start"], tabs["tail_groups"], tabs["n_used"], h2, meta, loc8)


def _ffn_kernel(te_ref, nu_ref, x_ref, wg_ref, wu_ref, wd_ref, y_ref, wg_s, wu_s, wd_s):
    j = pl.program_id(0)
    used = j < nu_ref[0]
    new_expert = jnp.logical_or(j == 0, te_ref[j] != te_ref[jnp.maximum(j - 1, 0)])

    @pl.when(jnp.logical_and(used, new_expert))
    def _():
        wg_s[...] = wg_ref[0].astype(BF16)
        wu_s[...] = wu_ref[0].astype(BF16)
        wd_s[...] = wd_ref[0].astype(BF16)

    @pl.when(used)
    def _():
        x = x_ref[...]
        g = jnp.dot(x, wg_s[...], preferred_element_type=F32)
        u = jnp.dot(x, wu_s[...], preferred_element_type=F32)
        y_ref[...] = jnp.dot((_silu(g) * u).astype(BF16), wd_s[...], preferred_element_type=F32).astype(BF16)

    @pl.when(jnp.logical_not(used))
    def _():
        y_ref[...] = jnp.zeros_like(y_ref)


def _ffn(xs, wg, wu, wd, layer, tabs, plan):
    d, de = wg.shape[2], wg.shape[3]
    tm = plan.ffn_tile
    row_blk = lambda j, te, nu: (jnp.minimum(j, nu[0] - 1), 0)
    grid_spec = pltpu.PrefetchScalarGridSpec(
        num_scalar_prefetch=2,
        grid=(plan.ffn_tiles,),
        in_specs=[pl.BlockSpec((tm, d), row_blk),
                  pl.BlockSpec((None, 1, d, de), lambda j, te, nu: (layer, te[j], 0, 0)),
                  pl.BlockSpec((None, 1, d, de), lambda j, te, nu: (layer, te[j], 0, 0)),
                  pl.BlockSpec((None, 1, de, d), lambda j, te, nu: (layer, te[j], 0, 0))],
        out_specs=pl.BlockSpec((tm, d), lambda j, te, nu: (j, 0)),
        scratch_shapes=[pltpu.VMEM((d, de), BF16), pltpu.VMEM((d, de), BF16), pltpu.VMEM((de, d), BF16)])
    return pl.pallas_call(
        _ffn_kernel,
        grid_spec=grid_spec,
        out_shape=jax.ShapeDtypeStruct(xs.shape, BF16),
        compiler_params=_cparams(("arbitrary",)),
        name="moe_ffn",
    )(tabs["tile_expert"], tabs["n_used"], xs, wg, wu, wd)


def _combine_kernel(dest_ref, ng_ref, ys_ref, cmeta_ref, x_ref, mod_ref, fw_ref, o_ref, buf_ref, sem,
                    *, plan, final):
    i = pl.program_id(0)
    n_tiles = pl.num_programs(0)
    slot = i % 2
    groups = plan.groups

    def fetch(tile, s):
        lax.fori_loop(0, ng_ref[tile],
                      lambda g, c: (_group_copy(buf_ref, s, g, ys_ref, dest_ref[tile * groups + g], sem, False).start(), c)[1],
                      0)

    @pl.when(i == 0)
    def _():
        buf_ref[...] = jnp.zeros_like(buf_ref)
        fetch(0, 0)

    @pl.when(i + 1 < n_tiles)
    def _():
        fetch(i + 1, 1 - slot)

    lax.fori_loop(0, ng_ref[i],
                  lambda g, c: (_group_copy(buf_ref, slot, 0, ys_ref, 0, sem, False).wait(), c)[1], 0)

    cm = cmeta_ref[...]
    rid = lax.broadcasted_iota(jnp.int32, (plan.tb, plan.local_rows), 1).astype(F32)
    comb = (jnp.where(rid == cm[:, 0:1], cm[:, 2:3], 0.0) + jnp.where(rid == cm[:, 1:2], cm[:, 3:4], 0.0)).astype(BF16)
    moe = jnp.dot(comb, buf_ref[slot], preferred_element_type=F32)
    xn = x_ref[...] + mod_ref[5:6, :] * moe
    if final:
        ms = jnp.mean(xn * xn, axis=-1, keepdims=True)
        xn = xn * lax.rsqrt(ms + EPS) * fw_ref[...]
    o_ref[...] = xn


def _combine(ys, cmeta, x1, mod4, layer, seq, tabs, plan, final_w, final):
    t, d = x1.shape
    per_b = seq // plan.tb
    grid_spec = pltpu.PrefetchScalarGridSpec(
        num_scalar_prefetch=2,
        grid=(plan.n_tiles,),
        in_specs=[pl.BlockSpec(memory_space=pl.ANY),
                  pl.BlockSpec((plan.tb, LANES), lambda i, *_: (i, 0)),
                  pl.BlockSpec((plan.tb, d), lambda i, *_: (i, 0)),
                  pl.BlockSpec((None, None, 6, d), lambda i, *_: (layer, i // per_b, 0, 0)),
                  pl.BlockSpec((1, d), lambda i, *_: (0, 0))],
        out_specs=pl.BlockSpec((plan.tb, d), lambda i, *_: (i, 0)),
        scratch_shapes=[pltpu.VMEM((2, plan.local_rows, d), BF16),
                        pltpu.SemaphoreType.DMA((2,))])
    return pl.pallas_call(
        functools.partial(_combine_kernel, plan=plan, final=final),
        grid_spec=grid_spec,
        out_shape=jax.ShapeDtypeStruct((t, d), F32),
        compiler_params=_cparams(("arbitrary",)),
        name="moe_combine",
    )(tabs["dest"], tabs["n_groups"], ys, cmeta, x1, mod4, final_w.reshape(1, d))


def kernel(x, c, w_ada, b_ada, norm_mix_w, norm_ffn_w, w_in_even, hg_lb_logits, hg_norm_w, ret_norm_w,
           w_out_even, w_in_odd, w_gla_a2, b_gla_a2, gla_norm_w, w_out_odd, w_router, b_router,
           w_gate, w_up, w_down, final_norm_w):
    bsz, seq, d = x.shape
    depth = w_ada.shape[0]
    x2 = x.reshape(bsz * seq, d)
    mod4 = _adaln_mod(c, w_ada, b_ada).reshape(depth, bsz, 6, d)
    wr_pad = jnp.pad(w_router, ((0, 0), (0, LANES - N_EXPERTS))).astype(BF16)
    br_pad = jnp.pad(b_router, (0, LANES - N_EXPERTS)).reshape(1, LANES)
    plan = _MoePlan(bsz * seq, min(MOE_TOKEN_TILE, seq))
    for l in range(depth):
        j = l // 2
        if l % 2 == 0:
            proj, hf = _inproj(x2, mod4, l, seq, norm_mix_w[l], w_in_even[j].astype(BF16),
                               _inproj_even_kernel, w_in_even.shape[-1], [], [])
            o = _even_mixers(proj, hf, hg_lb_logits, hg_norm_w[j], ret_norm_w[j], bsz, seq, l)
            w_out = w_out_even[j].astype(BF16)
        else:
            n_main = w_in_odd.shape[-1] - GLA_RANK
            wi = w_in_odd[j]
            wa = jnp.pad(wi[:, n_main:], ((0, 0), (0, LANES - GLA_RANK))).astype(BF16)
            wa2 = jnp.pad(w_gla_a2[j], ((0, LANES - GLA_RANK), (0, 0))).astype(BF16)
            full = lambda a: pl.BlockSpec(a.shape, lambda i: (0,) * a.ndim)
            ba2 = b_gla_a2[j].reshape(1, -1)
            proj, log_a = _inproj(x2, mod4, l, seq, norm_mix_w[l], wi[:, :n_main].astype(BF16),
                                  _inproj_odd_kernel, n_main, [full(wa), full(wa2), full(ba2)], [wa, wa2, ba2])
            o = _odd_mixer(proj, log_a, gla_norm_w[j], bsz, seq, l)
            w_out = w_out_odd[j].astype(BF16)
        x1, h2, meta, counts = _outproj_route(o, w_out, x2, mod4, l, seq, plan.tb, norm_ffn_w[l], wr_pad, br_pad)
        tabs = _moe_tables(counts[:, 0, :N_EXPERTS].astype(jnp.int32), plan)
        xs, cmeta = _dispatch(h2, meta, tabs, plan)
        ys = _ffn(xs, w_gate, w_up, w_down, l, tabs, plan)
        x2 = _combine(ys, cmeta, x1, mod4, l, seq, tabs, plan, final_norm_w, final=(l == depth - 1))
    return x2.reshape(bsz, seq, d)
```

```python
import functools

import jax
import jax.numpy as jnp
from jax import lax
from jax.experimental import pallas as pl
from jax.experimental.pallas import tpu as pltpu

F32 = jnp.float32
BF16 = jnp.bfloat16
HIGHEST = lax.Precision.HIGHEST

EPS = 1e-6
CHUNK = 64
HEAD_DIM = 128
N_HEADS = 4
N_EXPERTS = 16
N_GROUPS = 4
EXPERTS_PER_GROUP = N_EXPERTS // N_GROUPS
GLA_RANK = 16
GLA_GATE_NORM = 16.0
ROPE_BASE = 10000.0
LANES = 128
VMEM_LIMIT = 52 * 1024 * 1024
SUBLANES = 8
MIX_SEQ_TILE = 512
TOP_K = 2
ROW_GROUP = 16
MOE_TOKEN_TILE = 512
FFN_ROW_TILE = 512
PERM_BLOCK = 256
ROUTE_BLOCKS = 2


def _cparams(sem):
    return pltpu.CompilerParams(dimension_semantics=sem, vmem_limit_bytes=VMEM_LIMIT)


def _sigmoid(z):
    return 0.5 * jnp.tanh(0.5 * z) + 0.5


def _silu(z):
    return z * _sigmoid(z)


def _mod_kernel(c_ref, w_ref, b_ref, o_ref):
    c = c_ref[...]
    o_ref[0] = jnp.dot(_silu(c), w_ref[0], preferred_element_type=F32, precision=HIGHEST) + b_ref[0]


def _adaln_mod(c, w_ada, b_ada):
    depth, d, n = w_ada.shape
    bsz = c.shape[0]
    tn = 1536
    return pl.pallas_call(
        _mod_kernel,
        grid=(depth, n // tn),
        in_specs=[pl.BlockSpec((bsz, d), lambda l, j: (0, 0)),
                  pl.BlockSpec((1, d, tn), lambda l, j: (l, 0, j)),
                  pl.BlockSpec((1, 1, tn), lambda l, j: (l, 0, j))],
        out_specs=pl.BlockSpec((1, bsz, tn), lambda l, j: (l, 0, j)),
        out_shape=jax.ShapeDtypeStruct((depth, bsz, n), F32),
        compiler_params=_cparams(("arbitrary", "arbitrary")),
        name="adaln_mod",
    )(c, w_ada, b_ada.reshape(depth, 1, n))


def _norm_mod(x, nw, mod_ref, shift_row, scale_row):
    ms = jnp.mean(x * x, axis=-1, keepdims=True)
    return x * lax.rsqrt(ms + EPS) * (nw * (1.0 + mod_ref[scale_row:scale_row + 1, :])) + mod_ref[shift_row:shift_row + 1, :]


def _inproj_even_kernel(x_ref, mod_ref, nw_ref, w_ref, o_ref, f_ref):
    h = _norm_mod(x_ref[...], nw_ref[...], mod_ref, 0, 1)
    res = jnp.dot(h.astype(BF16), w_ref[...], preferred_element_type=F32)
    o_ref[...] = res.astype(BF16)
    f_ref[...] = res[:, 512:1024]


def _log_sigmoid(z):
    return jnp.minimum(z, 0.0) - jnp.log1p(jnp.exp(-jnp.abs(z)))


def _inproj_odd_kernel(x_ref, mod_ref, nw_ref, w_ref, wa_ref, wa2_ref, ba2_ref, o_ref, la_ref):
    h = _norm_mod(x_ref[...], nw_ref[...], mod_ref, 0, 1).astype(BF16)
    o_ref[...] = jnp.dot(h, w_ref[...], preferred_element_type=F32).astype(BF16)
    ga = jnp.dot(h, wa_ref[...], preferred_element_type=F32)
    z = jnp.dot(ga.astype(BF16), wa2_ref[...], preferred_element_type=F32) + ba2_ref[...]
    la_ref[...] = _log_sigmoid(z) * (1.0 / GLA_GATE_NORM)


def _inproj(x2, mod4, layer, seq, nw, weights, kernel_fn, n_main, extra_specs, extra_args):
    t, d = x2.shape
    tm = min(256, seq)
    per_b = seq // tm
    return pl.pallas_call(
        kernel_fn,
        grid=(t // tm,),
        in_specs=[pl.BlockSpec((tm, d), lambda i: (i, 0)),
                  pl.BlockSpec((None, None, 6, d), lambda i: (layer, i // per_b, 0, 0)),
                  pl.BlockSpec((1, d), lambda i: (0, 0)),
                  pl.BlockSpec((d, n_main), lambda i: (0, 0))] + extra_specs,
        out_specs=[pl.BlockSpec((tm, n_main), lambda i: (i, 0)),
                   pl.BlockSpec((tm, 512), lambda i: (i, 0))],
        out_shape=[jax.ShapeDtypeStruct((t, n_main), BF16),
                   jax.ShapeDtypeStruct((t, 512), F32)],
        compiler_params=_cparams(("parallel",)),
        name=f"inproj_l{layer}",
    )(x2, mod4, nw.reshape(1, d), weights, *extra_args)


def _causal_mask():
    r = lax.broadcasted_iota(jnp.int32, (CHUNK, CHUNK), 0)
    c = lax.broadcasted_iota(jnp.int32, (CHUNK, CHUNK), 1)
    return r >= c


def _dot_nt(a, b):
    return lax.dot_general(a, b, (((1,), (1,)), ((), ())), preferred_element_type=F32)


def _dot_tn(a, b):
    return lax.dot_general(a, b, (((0,), (0,)), ((), ())), preferred_element_type=F32)


def _head_out(o, g, nw, n_norm):
    ms = jnp.sum(o * o, axis=-1, keepdims=True) * (1.0 / n_norm)
    return o * lax.rsqrt(ms + EPS) * nw * _silu(g)


def _cumsum_rows(lg):
    tril = jnp.where(_causal_mask(), 1.0, 0.0).astype(BF16)
    hi = lg.astype(BF16)
    lo = (lg - hi.astype(F32)).astype(BF16)
    return jnp.dot(tril, hi, preferred_element_type=F32) + jnp.dot(tril, lo, preferred_element_type=F32)


def _decay_heads(q, k, v, b, st_ref, st_base, dv):
    b_mid = b[CHUNK // 2 - 1:CHUNK // 2, :]
    b_last = b[CHUNK - 1:CHUNK, :]
    qe = q * jnp.exp(b - b_mid)
    ke = k * jnp.exp(b_mid - b)
    qs = qe.astype(BF16)
    ks = ke.astype(BF16)
    qi = (qe * jnp.exp(b_mid)).astype(BF16)
    kd = (ke * jnp.exp(b_last - b_mid)).astype(BF16)
    dec = jnp.exp(b_last)
    mask = _causal_mask()
    outs = []
    for h in range(N_HEADS):
        kl = slice(h * HEAD_DIM, (h + 1) * HEAD_DIM)
        vh = v[:, h * dv:(h + 1) * dv]
        scores = jnp.where(mask, _dot_nt(qs[:, kl], ks[:, kl]), 0.0)
        o = jnp.dot(scores.astype(BF16), vh, preferred_element_type=F32)
        st = st_ref[st_base + h]
        o = o + _dot_nt(qi[:, kl], st.astype(BF16))
        st_ref[st_base + h] = dec[:, kl] * st + _dot_tn(vh, kd[:, kl])
        outs.append(o)
    return outs


def _even_mixer_kernel(proj_ref, hf_ref, lbl_ref, hgnw_ref, retnw_ref, cos_ref, sin_ref, o_ref, st_ref,
                       *, layer, n_chunks):
    hd, w = HEAD_DIM, N_HEADS * HEAD_DIM

    @pl.when(pl.program_id(1) == 0)
    def _():
        st_ref[...] = jnp.zeros_like(st_ref)

    lbl = lbl_ref[...]
    e = jnp.exp(lbl - jnp.max(lbl, axis=0, keepdims=True))
    p = e / jnp.sum(e, axis=0, keepdims=True)
    lb = jnp.sum(p[:layer + 1], axis=0, keepdims=True)
    hgnw = hgnw_ref[...]
    retnw = retnw_ref[...]
    headf = (lax.broadcasted_iota(jnp.int32, (1, w), 1) // hd).astype(F32)
    lg = jnp.log1p(-jnp.exp2(-5.0 - headf))
    pos = lax.broadcasted_iota(jnp.int32, (CHUNK, w), 0).astype(F32)
    e_q = jnp.exp((pos + 1.0) * lg)
    e_k = jnp.exp((CHUNK - 1.0 - pos) * lg)
    e_c = jnp.exp(CHUNK * lg)
    r = lax.broadcasted_iota(jnp.int32, (CHUNK, CHUNK), 0)
    cc = lax.broadcasted_iota(jnp.int32, (CHUNK, CHUNK), 1)
    dt = (r - cc).astype(F32)
    dmasks = [jnp.where(r >= cc, jnp.exp(dt * lg[:, h * hd:h * hd + CHUNK]), 0.0) for h in range(N_HEADS)]

    def rot(xx, cs, sn):
        return xx * cs + pltpu.roll(xx, hd // 2, 1) * sn

    def body(c, carry):
        sl = pl.ds(pl.multiple_of(c * CHUNK, CHUNK), CHUNK)
        f = lb + (1.0 - lb) * _sigmoid(hf_ref[sl, :])
        b = _cumsum_rows(jnp.log(f))
        q = _sigmoid(proj_ref[sl, 0:w].astype(F32))
        outs = _decay_heads(q, 1.0 - f, proj_ref[sl, 2 * w:3 * w], b, st_ref, 0, hd)
        for h, o in enumerate(outs):
            cl = slice(h * hd, (h + 1) * hd)
            g = proj_ref[sl, 3 * w + h * hd:3 * w + (h + 1) * hd].astype(F32)
            o_ref[sl, cl] = _head_out(o, g, hgnw[:, cl], hd).astype(BF16)
        cs = cos_ref[sl, :]
        sn = sin_ref[sl, :]
        for h in range(N_HEADS):
            cl = slice(h * hd, (h + 1) * hd)
            qh = rot(proj_ref[sl, 4 * w + h * hd:4 * w + (h + 1) * hd].astype(F32), cs, sn)
            kh = rot(proj_ref[sl, 5 * w + h * hd:5 * w + (h + 1) * hd].astype(F32), cs, sn) * (hd ** -0.5)
            vh = proj_ref[sl, 6 * w + h * hd:6 * w + (h + 1) * hd]
            scores = _dot_nt(qh.astype(BF16), kh.astype(BF16)) * dmasks[h]
            o = jnp.dot(scores.astype(BF16), vh, preferred_element_type=F32)
            st = st_ref[N_HEADS + h]
            o = o + _dot_nt((qh * e_q[:, cl]).astype(BF16), st.astype(BF16))
            st_ref[N_HEADS + h] = e_c[:, cl] * st + _dot_tn(vh, (kh * e_k[:, cl]).astype(BF16))
            g = proj_ref[sl, 7 * w + h * hd:7 * w + (h + 1) * hd].astype(F32)
            o_ref[sl, w + h * hd:w + (h + 1) * hd] = _head_out(o, g, retnw[:, cl], hd).astype(BF16)
        return carry

    lax.fori_loop(0, n_chunks, body, 0, unroll=True)


def _gla_mixer_kernel(proj_ref, la_ref, nw_ref, o_ref, st_ref, *, n_chunks):
    hd, w = HEAD_DIM, N_HEADS * HEAD_DIM
    dv = 2 * hd

    @pl.when(pl.program_id(1) == 0)
    def _():
        st_ref[...] = jnp.zeros_like(st_ref)

    nw = nw_ref[...]

    def body(c, carry):
        sl = pl.ds(pl.multiple_of(c * CHUNK, CHUNK), CHUNK)
        b = _cumsum_rows(la_ref[sl, :])
        q = proj_ref[sl, 0:w].astype(F32) * (hd ** -0.5)
        k = proj_ref[sl, w:2 * w].astype(F32)
        outs = _decay_heads(q, k, proj_ref[sl, 2 * w:2 * w + N_HEADS * dv], b, st_ref, 0, dv)
        for h, o in enumerate(outs):
            cl = slice(h * dv, (h + 1) * dv)
            g = proj_ref[sl, 4 * w + h * dv:4 * w + (h + 1) * dv].astype(F32)
            o_ref[sl, cl] = _head_out(o, g, nw[:, cl], dv).astype(BF16)
        return carry

    lax.fori_loop(0, n_chunks, body, 0, unroll=True)


def _mixer_call(kernel_fn, bsz, seq, ts, in_specs, n_states, dv, name, args):
    d_out = N_HEADS * 2 * HEAD_DIM
    return pl.pallas_call(
        kernel_fn,
        grid=(bsz, seq // ts),
        in_specs=in_specs,
        out_specs=pl.BlockSpec((ts, d_out), lambda b, s: (b * (seq // ts) + s, 0)),
        out_shape=jax.ShapeDtypeStruct((bsz * seq, d_out), BF16),
        scratch_shapes=[pltpu.VMEM((n_states, dv, HEAD_DIM), F32)],
        compiler_params=_cparams(("parallel", "arbitrary")),
        name=name,
    )(*args)


def _even_mixers(proj, hf, lb_logits, hg_nw, ret_nw, bsz, seq, layer):
    ts = min(MIX_SEQ_TILE, seq)
    per_b = seq // ts
    hd = HEAD_DIM
    rows = lambda width: pl.BlockSpec((ts, width), lambda b, s: (b * per_b + s, 0))
    full = lambda a: pl.BlockSpec(a.shape, lambda b, s: (0,) * a.ndim)
    inv = ROPE_BASE ** (-jnp.arange(0, hd, 2, dtype=F32) / hd)
    ang = jnp.arange(seq, dtype=F32)[:, None] * inv[None, :]
    cos2 = jnp.concatenate([jnp.cos(ang), jnp.cos(ang)], axis=-1)
    sin2 = jnp.concatenate([-jnp.sin(ang), jnp.sin(ang)], axis=-1)
    tab = pl.BlockSpec((ts, hd), lambda b, s: (s, 0))
    hg_nw, ret_nw = hg_nw.reshape(1, -1), ret_nw.reshape(1, -1)
    return _mixer_call(
        functools.partial(_even_mixer_kernel, layer=layer, n_chunks=ts // CHUNK), bsz, seq, ts,
        [rows(proj.shape[1]), rows(hf.shape[1]), full(lb_logits), full(hg_nw), full(ret_nw), tab, tab],
        2 * N_HEADS, hd, f"even_mixer_l{layer}",
        (proj, hf, lb_logits, hg_nw, ret_nw, cos2, sin2))


def _odd_mixer(proj, log_a, gla_nw, bsz, seq, layer):
    ts = min(MIX_SEQ_TILE, seq)
    per_b = seq // ts
    rows = lambda width: pl.BlockSpec((ts, width), lambda b, s: (b * per_b + s, 0))
    gla_nw = gla_nw.reshape(1, -1)
    return _mixer_call(
        functools.partial(_gla_mixer_kernel, n_chunks=ts // CHUNK), bsz, seq, ts,
        [rows(proj.shape[1]), rows(log_a.shape[1]), pl.BlockSpec(gla_nw.shape, lambda b, s: (0, 0))],
        N_HEADS, 2 * HEAD_DIM, f"gla_mixer_l{layer}",
        (proj, log_a, gla_nw))


def _lane_pick(lane, cols):
    out = jnp.zeros(lane.shape, F32)
    for j, col in reversed(list(enumerate(cols))):
        out = jnp.where(lane == j, col, out)
    return out


def _row_pick(n_rows, rows):
    rid = lax.broadcasted_iota(jnp.int32, (n_rows, rows[0].shape[1]), 0)
    out = jnp.zeros(rid.shape, F32)
    for j, row in reversed(list(enumerate(rows))):
        out = jnp.where(rid == j, row, out)
    return out


def _route(h2, wr_t, br, prior):
    neg = -jnp.inf
    logits = _dot_nt(wr_t, h2)
    tm = logits.shape[1]
    eid = lax.broadcasted_iota(jnp.int32, logits.shape, 0)
    ex = jnp.exp(logits - jnp.max(logits, axis=0, keepdims=True))
    probs = ex / jnp.sum(ex, axis=0, keepdims=True)
    sel = probs + br
    grp = eid // EXPERTS_PER_GROUP
    gs = [jnp.max(jnp.where(grp == g, sel, neg), axis=0, keepdims=True) for g in range(N_GROUPS)]
    best = functools.reduce(jnp.maximum, gs)
    gidx = jnp.full(best.shape, N_GROUPS - 1, jnp.int32)
    for g in range(N_GROUPS - 2, -1, -1):
        gidx = jnp.where(gs[g] == best, g, gidx)
    masked = jnp.where(grp == gidx, sel, neg)
    m1 = jnp.max(masked, axis=0, keepdims=True)
    i1 = jnp.min(jnp.where(masked == m1, eid, N_EXPERTS), axis=0, keepdims=True)
    masked2 = jnp.where(eid == i1, neg, masked)
    m2 = jnp.max(masked2, axis=0, keepdims=True)
    i2 = jnp.min(jnp.where(masked2 == m2, eid, N_EXPERTS), axis=0, keepdims=True)
    p1 = jnp.sum(jnp.where(eid == i1, probs, 0.0), axis=0, keepdims=True)
    p2 = jnp.sum(jnp.where(eid == i2, probs, 0.0), axis=0, keepdims=True)
    den = p1 + p2
    chosen = jnp.where(eid == i1, 1.0, jnp.where(eid == i2, 1.0, 0.0))
    r = lax.broadcasted_iota(jnp.int32, (tm, tm), 0)
    c = lax.broadcasted_iota(jnp.int32, (tm, tm), 1)
    earlier = jnp.where(r < c, 1.0, 0.0).astype(BF16)
    rank = jnp.dot(chosen.astype(BF16), earlier, preferred_element_type=F32) + prior
    rank1 = jnp.sum(jnp.where(eid == i1, rank, 0.0), axis=0, keepdims=True)
    rank2 = jnp.sum(jnp.where(eid == i2, rank, 0.0), axis=0, keepdims=True)
    meta = _row_pick(SUBLANES, [i1.astype(F32), i2.astype(F32), p1 / den, p2 / den, rank1, rank2])
    return meta, prior + jnp.sum(chosen, axis=1, keepdims=True)


def _outproj_kernel(o_ref, w_ref, x_ref, mod_ref, nw_ref, wr_ref, br_ref, x1_ref, h2_ref, meta_ref, cnt_ref):
    tb = x_ref.shape[0]
    blk = tb // ROUTE_BLOCKS
    counts = jnp.zeros((N_EXPERTS, 1), F32)
    for s in range(ROUTE_BLOCKS):
        rs = slice(s * blk, (s + 1) * blk)
        y = jnp.dot(o_ref[rs, :], w_ref[...], preferred_element_type=F32)
        x1 = x_ref[rs, :] + mod_ref[2:3, :] * y
        x1_ref[rs, :] = x1
        h2 = _norm_mod(x1, nw_ref[...], mod_ref, 3, 4).astype(BF16)
        h2_ref[rs, :] = h2
        meta, counts = _route(h2, wr_ref[...], br_ref[...], counts)
        meta_ref[:, rs] = meta
    cnt_ref[...] = jnp.broadcast_to(counts, cnt_ref.shape)


def _outproj_route(o, w, x2, mod4, layer, seq, tb, ffn_nw, wr_t, br_col):
    t, d = x2.shape
    per_b = seq // tb
    full = lambda a: pl.BlockSpec(a.shape, lambda i: (0,) * a.ndim)
    rows = lambda width: pl.BlockSpec((tb, width), lambda i: (i, 0))
    return pl.pallas_call(
        _outproj_kernel,
        grid=(t // tb,),
        in_specs=[rows(o.shape[1]), full(w), rows(d),
                  pl.BlockSpec((None, None, 6, d), lambda i: (layer, i // per_b, 0, 0)),
                  pl.BlockSpec((1, d), lambda i: (0, 0)), full(wr_t), full(br_col)],
        out_specs=[rows(d), rows(d),
                   pl.BlockSpec((None, SUBLANES, tb), lambda i: (i, 0, 0)),
                   pl.BlockSpec((None, N_EXPERTS, LANES), lambda i: (i, 0, 0))],
        out_shape=[jax.ShapeDtypeStruct((t, d), F32),
                   jax.ShapeDtypeStruct((t, d), BF16),
                   jax.ShapeDtypeStruct((t // tb, SUBLANES, tb), F32),
                   jax.ShapeDtypeStruct((t // tb, N_EXPERTS, LANES), F32)],
        compiler_params=_cparams(("parallel",)),
        name=f"outproj_route_l{layer}",
    )(o, w, x2, mod4, ffn_nw.reshape(1, d), wr_t, br_col)


class _MoePlan:
    def __init__(self, t, tb):
        self.tb = tb
        self.n_tiles = t // tb
        self.local_rows = -(-(TOP_K * tb + N_EXPERTS * (ROW_GROUP - 1)) // PERM_BLOCK) * PERM_BLOCK
        self.groups = self.local_rows // ROW_GROUP
        self.ffn_tile = min(FFN_ROW_TILE, TOP_K * tb)
        worst_rows = TOP_K * t + self.n_tiles * N_EXPERTS * (ROW_GROUP - 1)
        self.ffn_tiles = -(-worst_rows // self.ffn_tile) + N_EXPERTS
        self.sorted_rows = self.ffn_tiles * self.ffn_tile


def _moe_tables(counts, plan):
    i32 = jnp.int32
    n_pad = (counts + (ROW_GROUP - 1)) // ROW_GROUP * ROW_GROUP
    ends = jnp.cumsum(n_pad, axis=1)
    loc = ends - n_pad
    n_groups = ends[:, -1] // ROW_GROUP
    n_e = jnp.sum(n_pad, axis=0)
    n_e_pad = -(-n_e // plan.ffn_tile) * plan.ffn_tile
    region_end = jnp.cumsum(n_e_pad)
    base = region_end - n_e_pad
    goff = base[None, :] + jnp.cumsum(n_pad, axis=0) - n_pad
    n_used = region_end[-1] // plan.ffn_tile
    tile_row = jnp.minimum(jnp.arange(plan.ffn_tiles, dtype=i32), n_used - 1) * plan.ffn_tile
    tile_expert = jnp.minimum(jnp.sum(region_end[None, :] <= tile_row[:, None], axis=-1), N_EXPERTS - 1)
    flat = lambda a: a.reshape(-1).astype(i32)
    return dict(loc=loc, seg_src=flat(loc), seg_dst=flat(goff), seg_rows=flat(n_pad),
                tail_start=(base + n_e).astype(i32), tail_rows=(n_e_pad - n_e).astype(i32),
                tile_expert=tile_expert.astype(i32), n_used=n_used.reshape(1).astype(i32))


def _segment_copy(buf_ref, slot, src_row, hbm_ref, dst_row, n_rows, sem, to_hbm):
    al = lambda v: pl.multiple_of(v, ROW_GROUP)
    local = buf_ref.at[slot, pl.ds(al(src_row), al(n_rows))]
    remote = hbm_ref.at[pl.ds(al(dst_row), al(n_rows))]
    src, dst = (local, remote) if to_hbm else (remote, local)
    return pltpu.make_async_copy(src, dst, sem.at[slot])


def _tile_segments(tile, slot, buf_ref, hbm_ref, sem, src_ref, dst_ref, rows_ref, to_hbm, start):
    def seg(e, c):
        k = tile * N_EXPERTS + e
        n = rows_ref[k]

        @pl.when(n > 0)
        def _():
            cp = _segment_copy(buf_ref, slot, src_ref[k], hbm_ref, dst_ref[k], n, sem, to_hbm)
            cp.start() if start else cp.wait()
        return c

    lax.fori_loop(0, N_EXPERTS, seg, 0)


def _dispatch_kernel(src_ref, dst_ref, rows_ref, tstart_ref, trows_ref, nu_ref, h_ref, meta_ref, loc_ref,
                     sorted_ref, cmeta_ref, buf_ref, zero_ref, sem, zsem, *, plan):
    i = pl.program_id(0)
    last = pl.num_programs(0) - 1
    slot = i % 2
    tb, rows_l = plan.tb, plan.local_rows

    def tails(start):
        def tail(e, c):
            n = trows_ref[e]

            @pl.when(n > 0)
            def _():
                cp = pltpu.make_async_copy(zero_ref.at[pl.ds(0, pl.multiple_of(n, ROW_GROUP))],
                                           sorted_ref.at[pl.ds(pl.multiple_of(tstart_ref[e], ROW_GROUP),
                                                               pl.multiple_of(n, ROW_GROUP))], zsem.at[0])
                cp.start() if start else cp.wait()
            return c

        lax.fori_loop(0, N_EXPERTS, tail, 0)

    def spare_copy(j):
        row = j * plan.ffn_tile if isinstance(j, int) else pl.multiple_of(j * plan.ffn_tile, plan.ffn_tile)
        return pltpu.make_async_copy(zero_ref, sorted_ref.at[pl.ds(row, plan.ffn_tile)], zsem.at[1])

    @pl.when(i == 0)
    def _():
        zero_ref[...] = jnp.zeros_like(zero_ref)
        tails(True)
        lax.fori_loop(nu_ref[0], plan.ffn_tiles, lambda j, c: (spare_copy(j).start(), c)[1], 0)

    def wait_tile(tile, s):
        _tile_segments(tile, s, buf_ref, sorted_ref, sem, src_ref, dst_ref, rows_ref, True, False)

    @pl.when(i >= 2)
    def _():
        wait_tile(i - 2, slot)

    meta = meta_ref[...]
    eid = lax.broadcasted_iota(jnp.int32, (N_EXPERTS, tb), 0).astype(F32)
    loc_col = loc_ref[...]

    def local_row(e_row, rank_row):
        return jnp.sum(jnp.where(eid == e_row, loc_col, 0.0), axis=0, keepdims=True) + rank_row

    r1_row = local_row(meta[0:1, :], meta[4:5, :])
    r2_row = local_row(meta[1:2, :], meta[5:6, :])
    r1h, r2h = jnp.floor(r1_row * (1.0 / 256.0)), jnp.floor(r2_row * (1.0 / 256.0))
    pieces = _row_pick(LANES, [r1h, r1_row - 256.0 * r1h, r2h, r2_row - 256.0 * r2h,
                               meta[2:3, :], meta[3:4, :]]).astype(BF16)
    eye = jnp.where(lax.broadcasted_iota(jnp.int32, (tb, tb), 0) == lax.broadcasted_iota(jnp.int32, (tb, tb), 1),
                    1.0, 0.0).astype(BF16)
    cmeta_ref[...] = _dot_nt(eye, pieces)
    h = h_ref[...]
    blk = PERM_BLOCK
    for rb in range(rows_l // blk):
        rid = (lax.broadcasted_iota(jnp.int32, (blk, tb), 0) + rb * blk).astype(F32)
        perm = jnp.where(rid == r1_row, 1.0, jnp.where(rid == r2_row, 1.0, 0.0)).astype(BF16)
        buf_ref[slot, rb * blk:(rb + 1) * blk, :] = jnp.dot(perm, h, preferred_element_type=F32).astype(BF16)

    _tile_segments(i, slot, buf_ref, sorted_ref, sem, src_ref, dst_ref, rows_ref, True, True)

    @pl.when(i == last)
    def _():
        @pl.when(i >= 1)
        def _():
            wait_tile(i - 1, 1 - slot)
        wait_tile(i, slot)
        tails(False)
        lax.fori_loop(nu_ref[0], plan.ffn_tiles, lambda j, c: (spare_copy(0).wait(), c)[1], 0)


def _dispatch(h2, meta, tabs, plan):
    t, d = h2.shape
    loc_col = tabs["loc"].astype(F32)[:, :, None]
    grid_spec = pltpu.PrefetchScalarGridSpec(
        num_scalar_prefetch=6,
        grid=(plan.n_tiles,),
        in_specs=[pl.BlockSpec((plan.tb, d), lambda i, *_: (i, 0)),
                  pl.BlockSpec((None, SUBLANES, plan.tb), lambda i, *_: (i, 0, 0)),
                  pl.BlockSpec((None, N_EXPERTS, 1), lambda i, *_: (i, 0, 0))],
        out_specs=[pl.BlockSpec(memory_space=pl.ANY),
                   pl.BlockSpec((plan.tb, LANES), lambda i, *_: (i, 0))],
        scratch_shapes=[pltpu.VMEM((2, plan.local_rows, d), BF16),
                        pltpu.VMEM((plan.ffn_tile, d), BF16),
                        pltpu.SemaphoreType.DMA((2,)),
                        pltpu.SemaphoreType.DMA((2,))])
    return pl.pallas_call(
        functools.partial(_dispatch_kernel, plan=plan),
        grid_spec=grid_spec,
        out_shape=[jax.ShapeDtypeStruct((plan.sorted_rows, d), BF16),
                   jax.ShapeDtypeStruct((t, LANES), F32)],
        compiler_params=_cparams(("arbitrary",)),
        name="moe_dispatch",
    )(tabs["seg_src"], tabs["seg_dst"], tabs["seg_rows"], tabs["tail_start"], tabs["tail_rows"], tabs["n_used"],
      h2, meta, loc_col)


def _ffn_kernel(te_ref, nu_ref, x_ref, wg_ref, wu_ref, wd_ref, y_ref, wg_s, wu_s, wd_s):
    j = pl.program_id(0)
    used = j < nu_ref[0]
    new_expert = jnp.logical_or(j == 0, te_ref[j] != te_ref[jnp.maximum(j - 1, 0)])

    @pl.when(jnp.logical_and(used, new_expert))
    def _():
        wg_s[...] = wg_ref[0].astype(BF16)
        wu_s[...] = wu_ref[0].astype(BF16)
        wd_s[...] = wd_ref[0].astype(BF16)

    @pl.when(used)
    def _():
        x = x_ref[...]
        g = jnp.dot(x, wg_s[...], preferred_element_type=F32)
        u = jnp.dot(x, wu_s[...], preferred_element_type=F32)
        y_ref[...] = jnp.dot((_silu(g) * u).astype(BF16), wd_s[...], preferred_element_type=F32).astype(BF16)

    @pl.when(jnp.logical_not(used))
    def _():
        y_ref[...] = jnp.zeros_like(y_ref)


def _ffn(xs, wg, wu, wd, layer, tabs, plan):
    d, de = wg.shape[2], wg.shape[3]
    tm = plan.ffn_tile
    row_blk = lambda j, te, nu: (jnp.minimum(j, nu[0] - 1), 0)
    grid_spec = pltpu.PrefetchScalarGridSpec(
        num_scalar_prefetch=2,
        grid=(plan.ffn_tiles,),
        in_specs=[pl.BlockSpec((tm, d), row_blk),
                  pl.BlockSpec((None, 1, d, de), lambda j, te, nu: (layer, te[j], 0, 0)),
                  pl.BlockSpec((None, 1, d, de), lambda j, te, nu: (layer, te[j], 0, 0)),
                  pl.BlockSpec((None, 1, de, d), lambda j, te, nu: (layer, te[j], 0, 0))],
        out_specs=pl.BlockSpec((tm, d), lambda j, te, nu: (j, 0)),
        scratch_shapes=[pltpu.VMEM((d, de), BF16), pltpu.VMEM((d, de), BF16), pltpu.VMEM((de, d), BF16)])
    return pl.pallas_call(
        _ffn_kernel,
        grid_spec=grid_spec,
        out_shape=jax.ShapeDtypeStruct(xs.shape, BF16),
        compiler_params=_cparams(("arbitrary",)),
        name="moe_ffn",
    )(tabs["tile_expert"], tabs["n_used"], xs, wg, wu, wd)


def _combine_kernel(src_ref, dst_ref, rows_ref, ys_ref, cmeta_ref, x_ref, mod_ref, fw_ref, o_ref, buf_ref, sem,
                    *, plan, final):
    i = pl.program_id(0)
    n_tiles = pl.num_programs(0)
    slot = i % 2

    def fetch(tile, s, start):
        _tile_segments(tile, s, buf_ref, ys_ref, sem, src_ref, dst_ref, rows_ref, False, start)

    @pl.when(i == 0)
    def _():
        buf_ref[...] = jnp.zeros_like(buf_ref)
        fetch(0, 0, True)

    @pl.when(i + 1 < n_tiles)
    def _():
        fetch(i + 1, 1 - slot, True)

    fetch(i, slot, False)

    cm = cmeta_ref[...]
    r1 = cm[:, 0:1] * 256.0 + cm[:, 1:2]
    r2 = cm[:, 2:3] * 256.0 + cm[:, 3:4]
    rid = lax.broadcasted_iota(jnp.int32, (plan.tb, plan.local_rows), 1).astype(F32)
    comb = (jnp.where(rid == r1, cm[:, 4:5], 0.0) + jnp.where(rid == r2, cm[:, 5:6], 0.0)).astype(BF16)
    moe = jnp.dot(comb, buf_ref[slot], preferred_element_type=F32)
    xn = x_ref[...] + mod_ref[5:6, :] * moe
    if final:
        ms = jnp.mean(xn * xn, axis=-1, keepdims=True)
        xn = xn * lax.rsqrt(ms + EPS) * fw_ref[...]
    o_ref[...] = xn


def _combine(ys, cmeta, x1, mod4, layer, seq, tabs, plan, final_w, final):
    t, d = x1.shape
    per_b = seq // plan.tb
    grid_spec = pltpu.PrefetchScalarGridSpec(
        num_scalar_prefetch=3,
        grid=(plan.n_tiles,),
        in_specs=[pl.BlockSpec(memory_space=pl.ANY),
                  pl.BlockSpec((plan.tb, LANES), lambda i, *_: (i, 0)),
                  pl.BlockSpec((plan.tb, d), lambda i, *_: (i, 0)),
                  pl.BlockSpec((None, None, 6, d), lambda i, *_: (layer, i // per_b, 0, 0)),
                  pl.BlockSpec((1, d), lambda i, *_: (0, 0))],
        out_specs=pl.BlockSpec((plan.tb, d), lambda i, *_: (i, 0)),
        scratch_shapes=[pltpu.VMEM((2, plan.local_rows, d), BF16),
                        pltpu.SemaphoreType.DMA((2,))])
    return pl.pallas_call(
        functools.partial(_combine_kernel, plan=plan, final=final),
        grid_spec=grid_spec,
        out_shape=jax.ShapeDtypeStruct((t, d), F32),
        compiler_params=_cparams(("arbitrary",)),
        name="moe_combine",
    )(tabs["seg_src"], tabs["seg_dst"], tabs["seg_rows"], ys, cmeta, x1, mod4, final_w.reshape(1, d))


def kernel(x, c, w_ada, b_ada, norm_mix_w, norm_ffn_w, w_in_even, hg_lb_logits, hg_norm_w, ret_norm_w,
           w_out_even, w_in_odd, w_gla_a2, b_gla_a2, gla_norm_w, w_out_odd, w_router, b_router,
           w_gate, w_up, w_down, final_norm_w):
    bsz, seq, d = x.shape
    depth = w_ada.shape[0]
    x2 = x.reshape(bsz * seq, d)
    mod4 = _adaln_mod(c, w_ada, b_ada).reshape(depth, bsz, 6, d)
    wr_t = w_router.T.astype(BF16)
    br_col = b_router.reshape(N_EXPERTS, 1)
    plan = _MoePlan(bsz * seq, min(MOE_TOKEN_TILE, seq))
    for l in range(depth):
        j = l // 2
        if l % 2 == 0:
            proj, hf = _inproj(x2, mod4, l, seq, norm_mix_w[l], w_in_even[j].astype(BF16),
                               _inproj_even_kernel, w_in_even.shape[-1], [], [])
            o = _even_mixers(proj, hf, hg_lb_logits, hg_norm_w[j], ret_norm_w[j], bsz, seq, l)
            w_out = w_out_even[j].astype(BF16)
        else:
            n_main = w_in_odd.shape[-1] - GLA_RANK
            wi = w_in_odd[j]
            wa = jnp.pad(wi[:, n_main:], ((0, 0), (0, LANES - GLA_RANK))).astype(BF16)
            wa2 = jnp.pad(w_gla_a2[j], ((0, LANES - GLA_RANK), (0, 0))).astype(BF16)
            full = lambda a: pl.BlockSpec(a.shape, lambda i: (0,) * a.ndim)
            ba2 = b_gla_a2[j].reshape(1, -1)
            proj, log_a = _inproj(x2, mod4, l, seq, norm_mix_w[l], wi[:, :n_main].astype(BF16),
                                  _inproj_odd_kernel, n_main, [full(wa), full(wa2), full(ba2)], [wa, wa2, ba2])
            o = _odd_mixer(proj, log_a, gla_norm_w[j], bsz, seq, l)
            w_out = w_out_odd[j].astype(BF16)
        x1, h2, meta, counts = _outproj_route(o, w_out, x2, mod4, l, seq, plan.tb, norm_ffn_w[l], wr_t, br_col)
        tabs = _moe_tables(counts[:, :, 0].astype(jnp.int32), plan)
        xs, cmeta = _dispatch(h2, meta, tabs, plan)
        ys = _ffn(xs, w_gate, w_up, w_down, l, tabs, plan)
        x2 = _combine(ys, cmeta, x1, mod4, l, seq, tabs, plan, final_norm_w, final=(l == depth - 1))
    return x2.reshape(bsz, seq, d)
```

```python
import functools

import jax
import jax.numpy as jnp
from jax import lax
from jax.experimental import pallas as pl
from jax.experimental.pallas import tpu as pltpu

F32 = jnp.float32
BF16 = jnp.bfloat16
HIGHEST = lax.Precision.HIGHEST

EPS = 1e-6
CHUNK = 64
RET_CHUNK = 256
HEAD_DIM = 128
N_HEADS = 4
N_EXPERTS = 16
N_GROUPS = 4
EXPERTS_PER_GROUP = N_EXPERTS // N_GROUPS
GLA_RANK = 16
GLA_GATE_NORM = 16.0
ROPE_BASE = 10000.0
LANES = 128
VMEM_LIMIT = 52 * 1024 * 1024
SUBLANES = 8
INPROJ_ROW_TILE = 512
MIX_SEQ_TILE = 512
TOP_K = 2
ROW_GROUP = 16
MOE_TOKEN_TILE = 512
FFN_ROW_TILE = 512
PERM_BLOCK = 256
ROUTE_BLOCKS = 2


def _cparams(sem):
    return pltpu.CompilerParams(dimension_semantics=sem, vmem_limit_bytes=VMEM_LIMIT)


def _sigmoid(z):
    return 0.5 * jnp.tanh(0.5 * z) + 0.5


def _silu(z):
    return z * _sigmoid(z)


def _mod_kernel(c_ref, w_ref, b_ref, o_ref):
    c = c_ref[...]
    o_ref[0] = jnp.dot(_silu(c), w_ref[0], preferred_element_type=F32, precision=HIGHEST) + b_ref[0]


def _adaln_mod(c, w_ada, b_ada):
    depth, d, n = w_ada.shape
    bsz = c.shape[0]
    tn = 1536
    return pl.pallas_call(
        _mod_kernel,
        grid=(depth, n // tn),
        in_specs=[pl.BlockSpec((bsz, d), lambda l, j: (0, 0)),
                  pl.BlockSpec((1, d, tn), lambda l, j: (l, 0, j)),
                  pl.BlockSpec((1, 1, tn), lambda l, j: (l, 0, j))],
        out_specs=pl.BlockSpec((1, bsz, tn), lambda l, j: (l, 0, j)),
        out_shape=jax.ShapeDtypeStruct((depth, bsz, n), F32),
        compiler_params=_cparams(("arbitrary", "arbitrary")),
        name="adaln_mod",
    )(c, w_ada, b_ada.reshape(depth, 1, n))


def _norm_mod(x, nw, mod_ref, shift_row, scale_row):
    ms = jnp.mean(x * x, axis=-1, keepdims=True)
    return x * lax.rsqrt(ms + EPS) * (nw * (1.0 + mod_ref[scale_row:scale_row + 1, :])) + mod_ref[shift_row:shift_row + 1, :]


def _inproj_even_kernel(x_ref, mod_ref, nw_ref, w_ref, o_ref, f_ref):
    h = _norm_mod(x_ref[...], nw_ref[...], mod_ref, 0, 1)
    res = jnp.dot(h.astype(BF16), w_ref[...], preferred_element_type=F32)
    o_ref[...] = res.astype(BF16)
    f_ref[...] = res[:, 512:1024]


def _log_sigmoid(z):
    return jnp.minimum(z, 0.0) - jnp.log1p(jnp.exp(-jnp.abs(z)))


def _inproj_odd_kernel(x_ref, mod_ref, nw_ref, w_ref, wa_ref, wa2_ref, ba2_ref, o_ref, la_ref):
    h = _norm_mod(x_ref[...], nw_ref[...], mod_ref, 0, 1).astype(BF16)
    o_ref[...] = jnp.dot(h, w_ref[...], preferred_element_type=F32).astype(BF16)
    ga = jnp.dot(h, wa_ref[...], preferred_element_type=F32)
    z = jnp.dot(ga.astype(BF16), wa2_ref[...], preferred_element_type=F32) + ba2_ref[...]
    la_ref[...] = _log_sigmoid(z) * (1.0 / GLA_GATE_NORM)


def _inproj(x2, mod4, layer, seq, nw, weights, kernel_fn, n_main, extra_specs, extra_args):
    t, d = x2.shape
    tm = min(INPROJ_ROW_TILE, seq)
    per_b = seq // tm
    return pl.pallas_call(
        kernel_fn,
        grid=(t // tm,),
        in_specs=[pl.BlockSpec((tm, d), lambda i: (i, 0)),
                  pl.BlockSpec((None, None, 6, d), lambda i: (layer, i // per_b, 0, 0)),
                  pl.BlockSpec((1, d), lambda i: (0, 0)),
                  pl.BlockSpec((d, n_main), lambda i: (0, 0))] + extra_specs,
        out_specs=[pl.BlockSpec((tm, n_main), lambda i: (i, 0)),
                   pl.BlockSpec((tm, 512), lambda i: (i, 0))],
        out_shape=[jax.ShapeDtypeStruct((t, n_main), BF16),
                   jax.ShapeDtypeStruct((t, 512), F32)],
        compiler_params=_cparams(("parallel",)),
        name=f"inproj_l{layer}",
    )(x2, mod4, nw.reshape(1, d), weights, *extra_args)


def _causal_mask():
    r = lax.broadcasted_iota(jnp.int32, (CHUNK, CHUNK), 0)
    c = lax.broadcasted_iota(jnp.int32, (CHUNK, CHUNK), 1)
    return r >= c


def _dot_nt(a, b):
    return lax.dot_general(a, b, (((1,), (1,)), ((), ())), preferred_element_type=F32)


def _dot_tn(a, b):
    return lax.dot_general(a, b, (((0,), (0,)), ((), ())), preferred_element_type=F32)


def _head_out(o, g, nw, n_norm):
    ms = jnp.sum(o * o, axis=-1, keepdims=True) * (1.0 / n_norm)
    return o * lax.rsqrt(ms + EPS) * nw * _silu(g)


def _cumsum_rows(lg):
    tril = jnp.where(_causal_mask(), 1.0, 0.0).astype(BF16)
    hi = lg.astype(BF16)
    lo = (lg - hi.astype(F32)).astype(BF16)
    return jnp.dot(tril, hi, preferred_element_type=F32) + jnp.dot(tril, lo, preferred_element_type=F32)


def _decay_heads(q, k, v, b, st_ref, st_base, dv):
    b_mid = b[CHUNK // 2 - 1:CHUNK // 2, :]
    b_last = b[CHUNK - 1:CHUNK, :]
    qe = q * jnp.exp(b - b_mid)
    ke = k * jnp.exp(b_mid - b)
    qs = qe.astype(BF16)
    ks = ke.astype(BF16)
    qi = (qe * jnp.exp(b_mid)).astype(BF16)
    kd = (ke * jnp.exp(b_last - b_mid)).astype(BF16)
    dec = jnp.exp(b_last)
    mask = _causal_mask()
    outs = []
    for h in range(N_HEADS):
        kl = slice(h * HEAD_DIM, (h + 1) * HEAD_DIM)
        vh = v[:, h * dv:(h + 1) * dv]
        scores = jnp.where(mask, _dot_nt(qs[:, kl], ks[:, kl]), 0.0)
        o = jnp.dot(scores.astype(BF16), vh, preferred_element_type=F32)
        st = st_ref[st_base + h]
        o = o + _dot_nt(qi[:, kl], st.astype(BF16))
        st_ref[st_base + h] = dec[:, kl] * st + _dot_tn(vh, kd[:, kl])
        outs.append(o)
    return outs


def _retention_log_decay():
    headf = (lax.broadcasted_iota(jnp.int32, (1, N_HEADS * HEAD_DIM), 1) // HEAD_DIM).astype(F32)
    return jnp.log1p(-jnp.exp2(-5.0 - headf))


def _even_mixer_kernel(proj_ref, hf_ref, lbl_ref, hgnw_ref, retnw_ref, cos_ref, sin_ref, o_ref,
                       st_ref, eq_ref, ek_ref, dm_ref, *, layer, n_chunks, ret_chunk):
    hd, w = HEAD_DIM, N_HEADS * HEAD_DIM
    lg = _retention_log_decay()

    @pl.when(jnp.logical_and(pl.program_id(0) == 0, pl.program_id(1) == 0))
    def _():
        pos = lax.broadcasted_iota(jnp.int32, (ret_chunk, w), 0).astype(F32)
        eq_ref[...] = jnp.exp((pos + 1.0) * lg)
        ek_ref[...] = jnp.exp((ret_chunk - 1.0 - pos) * lg)
        r = lax.broadcasted_iota(jnp.int32, (ret_chunk, ret_chunk), 0)
        cc = lax.broadcasted_iota(jnp.int32, (ret_chunk, ret_chunk), 1)
        dt = (r - cc).astype(F32)
        for h in range(N_HEADS):
            dm_ref[h] = jnp.where(r >= cc, jnp.exp(dt * lg[:, h * hd:h * hd + 1]), 0.0)

    @pl.when(pl.program_id(1) == 0)
    def _():
        st_ref[...] = jnp.zeros_like(st_ref)

    lbl = lbl_ref[...]
    e = jnp.exp(lbl - jnp.max(lbl, axis=0, keepdims=True))
    p = e / jnp.sum(e, axis=0, keepdims=True)
    lb = jnp.sum(p[:layer + 1], axis=0, keepdims=True)
    hgnw = hgnw_ref[...]
    retnw = retnw_ref[...]
    e_c = jnp.exp(ret_chunk * lg)

    def rot(xx, cs, sn):
        return xx * cs + pltpu.roll(xx, hd // 2, 1) * sn

    for c in range(n_chunks):
        sl = pl.ds(c * CHUNK, CHUNK)
        f = lb + (1.0 - lb) * _sigmoid(hf_ref[sl, :])
        b = _cumsum_rows(jnp.log(f))
        q = _sigmoid(proj_ref[sl, 0:w].astype(F32))
        outs = _decay_heads(q, 1.0 - f, proj_ref[sl, 2 * w:3 * w], b, st_ref, 0, hd)
        for h, o in enumerate(outs):
            cl = slice(h * hd, (h + 1) * hd)
            g = proj_ref[sl, 3 * w + h * hd:3 * w + (h + 1) * hd].astype(F32)
            o_ref[sl, cl] = _head_out(o, g, hgnw[:, cl], hd).astype(BF16)

    for c in range(n_chunks * CHUNK // ret_chunk):
        sl = pl.ds(c * ret_chunk, ret_chunk)
        cs = cos_ref[sl, :]
        sn = sin_ref[sl, :]
        for h in range(N_HEADS):
            cl = slice(h * hd, (h + 1) * hd)
            qh = rot(proj_ref[sl, 4 * w + h * hd:4 * w + (h + 1) * hd].astype(F32), cs, sn)
            kh = rot(proj_ref[sl, 5 * w + h * hd:5 * w + (h + 1) * hd].astype(F32), cs, sn) * (hd ** -0.5)
            vh = proj_ref[sl, 6 * w + h * hd:6 * w + (h + 1) * hd]
            scores = _dot_nt(qh.astype(BF16), kh.astype(BF16)) * dm_ref[h]
            o = jnp.dot(scores.astype(BF16), vh, preferred_element_type=F32)
            st = st_ref[N_HEADS + h]
            o = o + _dot_nt((qh * eq_ref[:, cl]).astype(BF16), st.astype(BF16))
            st_ref[N_HEADS + h] = e_c[:, cl] * st + _dot_tn(vh, (kh * ek_ref[:, cl]).astype(BF16))
            g = proj_ref[sl, 7 * w + h * hd:7 * w + (h + 1) * hd].astype(F32)
            o_ref[sl, w + h * hd:w + (h + 1) * hd] = _head_out(o, g, retnw[:, cl], hd).astype(BF16)


def _gla_mixer_kernel(proj_ref, la_ref, nw_ref, o_ref, st_ref, *, n_chunks):
    hd, w = HEAD_DIM, N_HEADS * HEAD_DIM
    dv = 2 * hd

    @pl.when(pl.program_id(1) == 0)
    def _():
        st_ref[...] = jnp.zeros_like(st_ref)

    nw = nw_ref[...]
    for c in range(n_chunks):
        sl = pl.ds(c * CHUNK, CHUNK)
        b = _cumsum_rows(la_ref[sl, :])
        q = proj_ref[sl, 0:w].astype(F32) * (hd ** -0.5)
        k = proj_ref[sl, w:2 * w].astype(F32)
        outs = _decay_heads(q, k, proj_ref[sl, 2 * w:2 * w + N_HEADS * dv], b, st_ref, 0, dv)
        for h, o in enumerate(outs):
            cl = slice(h * dv, (h + 1) * dv)
            g = proj_ref[sl, 4 * w + h * dv:4 * w + (h + 1) * dv].astype(F32)
            o_ref[sl, cl] = _head_out(o, g, nw[:, cl], dv).astype(BF16)


def _mixer_call(kernel_fn, bsz, seq, ts, in_specs, scratch_shapes, name, args):
    d_out = N_HEADS * 2 * HEAD_DIM
    return pl.pallas_call(
        kernel_fn,
        grid=(bsz, seq // ts),
        in_specs=in_specs,
        out_specs=pl.BlockSpec((ts, d_out), lambda b, s: (b * (seq // ts) + s, 0)),
        out_shape=jax.ShapeDtypeStruct((bsz * seq, d_out), BF16),
        scratch_shapes=scratch_shapes,
        compiler_params=_cparams(("arbitrary", "arbitrary")),
        name=name,
    )(*args)


def _even_mixers(proj, hf, lb_logits, hg_nw, ret_nw, bsz, seq, layer):
    ts = min(MIX_SEQ_TILE, seq)
    rc = min(RET_CHUNK, ts)
    per_b = seq // ts
    hd, w = HEAD_DIM, N_HEADS * HEAD_DIM
    rows = lambda width: pl.BlockSpec((ts, width), lambda b, s: (b * per_b + s, 0))
    full = lambda a: pl.BlockSpec(a.shape, lambda b, s: (0,) * a.ndim)
    inv = ROPE_BASE ** (-jnp.arange(0, hd, 2, dtype=F32) / hd)
    ang = jnp.arange(seq, dtype=F32)[:, None] * inv[None, :]
    cos2 = jnp.concatenate([jnp.cos(ang), jnp.cos(ang)], axis=-1)
    sin2 = jnp.concatenate([-jnp.sin(ang), jnp.sin(ang)], axis=-1)
    tab = pl.BlockSpec((ts, hd), lambda b, s: (s, 0))
    hg_nw, ret_nw = hg_nw.reshape(1, -1), ret_nw.reshape(1, -1)
    scratch = [pltpu.VMEM((2 * N_HEADS, hd, hd), F32), pltpu.VMEM((rc, w), F32), pltpu.VMEM((rc, w), F32),
               pltpu.VMEM((N_HEADS, rc, rc), F32)]
    return _mixer_call(
        functools.partial(_even_mixer_kernel, layer=layer, n_chunks=ts // CHUNK, ret_chunk=rc), bsz, seq, ts,
        [rows(proj.shape[1]), rows(hf.shape[1]), full(lb_logits), full(hg_nw), full(ret_nw), tab, tab],
        scratch, f"even_mixer_l{layer}",
        (proj, hf, lb_logits, hg_nw, ret_nw, cos2, sin2))


def _odd_mixer(proj, log_a, gla_nw, bsz, seq, layer):
    ts = min(MIX_SEQ_TILE, seq)
    per_b = seq // ts
    rows = lambda width: pl.BlockSpec((ts, width), lambda b, s: (b * per_b + s, 0))
    gla_nw = gla_nw.reshape(1, -1)
    return _mixer_call(
        functools.partial(_gla_mixer_kernel, n_chunks=ts // CHUNK), bsz, seq, ts,
        [rows(proj.shape[1]), rows(log_a.shape[1]), pl.BlockSpec(gla_nw.shape, lambda b, s: (0, 0))],
        [pltpu.VMEM((N_HEADS, 2 * HEAD_DIM, HEAD_DIM), F32)], f"gla_mixer_l{layer}",
        (proj, log_a, gla_nw))


def _row_pick(n_rows, rows):
    rid = lax.broadcasted_iota(jnp.int32, (n_rows, rows[0].shape[1]), 0)
    out = jnp.zeros(rid.shape, F32)
    for j, row in reversed(list(enumerate(rows))):
        out = jnp.where(rid == j, row, out)
    return out


def _route(h2, wr_t, br, prior):
    neg = -jnp.inf
    logits = _dot_nt(wr_t, h2)
    tm = logits.shape[1]
    eid = lax.broadcasted_iota(jnp.int32, logits.shape, 0)
    ex = jnp.exp(logits - jnp.max(logits, axis=0, keepdims=True))
    probs = ex / jnp.sum(ex, axis=0, keepdims=True)
    sel = probs + br
    grp = eid // EXPERTS_PER_GROUP
    gs = [jnp.max(jnp.where(grp == g, sel, neg), axis=0, keepdims=True) for g in range(N_GROUPS)]
    best = functools.reduce(jnp.maximum, gs)
    gidx = jnp.full(best.shape, N_GROUPS - 1, jnp.int32)
    for g in range(N_GROUPS - 2, -1, -1):
        gidx = jnp.where(gs[g] == best, g, gidx)
    masked = jnp.where(grp == gidx, sel, neg)
    m1 = jnp.max(masked, axis=0, keepdims=True)
    i1 = jnp.min(jnp.where(masked == m1, eid, N_EXPERTS), axis=0, keepdims=True)
    masked2 = jnp.where(eid == i1, neg, masked)
    m2 = jnp.max(masked2, axis=0, keepdims=True)
    i2 = jnp.min(jnp.where(masked2 == m2, eid, N_EXPERTS), axis=0, keepdims=True)
    p1 = jnp.sum(jnp.where(eid == i1, probs, 0.0), axis=0, keepdims=True)
    p2 = jnp.sum(jnp.where(eid == i2, probs, 0.0), axis=0, keepdims=True)
    den = p1 + p2
    chosen = jnp.where(eid == i1, 1.0, jnp.where(eid == i2, 1.0, 0.0))
    r = lax.broadcasted_iota(jnp.int32, (tm, tm), 0)
    c = lax.broadcasted_iota(jnp.int32, (tm, tm), 1)
    earlier = jnp.where(r < c, 1.0, 0.0).astype(BF16)
    rank = jnp.dot(chosen.astype(BF16), earlier, preferred_element_type=F32) + prior
    rank1 = jnp.sum(jnp.where(eid == i1, rank, 0.0), axis=0, keepdims=True)
    rank2 = jnp.sum(jnp.where(eid == i2, rank, 0.0), axis=0, keepdims=True)
    meta = _row_pick(SUBLANES, [i1.astype(F32), i2.astype(F32), p1 / den, p2 / den, rank1, rank2])
    return meta, prior + jnp.sum(chosen, axis=1, keepdims=True)


def _outproj_kernel(o_ref, w_ref, x_ref, mod_ref, nw_ref, wr_ref, br_ref, x1_ref, h2_ref, meta_ref, cnt_ref):
    tb = x_ref.shape[0]
    blk = tb // ROUTE_BLOCKS
    counts = jnp.zeros((N_EXPERTS, 1), F32)
    for s in range(ROUTE_BLOCKS):
        rs = slice(s * blk, (s + 1) * blk)
        y = jnp.dot(o_ref[rs, :], w_ref[...], preferred_element_type=F32)
        x1 = x_ref[rs, :] + mod_ref[2:3, :] * y
        x1_ref[rs, :] = x1
        h2 = _norm_mod(x1, nw_ref[...], mod_ref, 3, 4).astype(BF16)
        h2_ref[rs, :] = h2
        meta, counts = _route(h2, wr_ref[...], br_ref[...], counts)
        meta_ref[:, rs] = meta
    cnt_ref[...] = jnp.broadcast_to(counts, cnt_ref.shape)


def _outproj_route(o, w, x2, mod4, layer, seq, tb, ffn_nw, wr_t, br_col):
    t, d = x2.shape
    per_b = seq // tb
    full = lambda a: pl.BlockSpec(a.shape, lambda i: (0,) * a.ndim)
    rows = lambda width: pl.BlockSpec((tb, width), lambda i: (i, 0))
    return pl.pallas_call(
        _outproj_kernel,
        grid=(t // tb,),
        in_specs=[rows(o.shape[1]), full(w), rows(d),
                  pl.BlockSpec((None, None, 6, d), lambda i: (layer, i // per_b, 0, 0)),
                  pl.BlockSpec((1, d), lambda i: (0, 0)), full(wr_t), full(br_col)],
        out_specs=[rows(d), rows(d),
                   pl.BlockSpec((None, SUBLANES, tb), lambda i: (i, 0, 0)),
                   pl.BlockSpec((None, N_EXPERTS, LANES), lambda i: (i, 0, 0))],
        out_shape=[jax.ShapeDtypeStruct((t, d), F32),
                   jax.ShapeDtypeStruct((t, d), BF16),
                   jax.ShapeDtypeStruct((t // tb, SUBLANES, tb), F32),
                   jax.ShapeDtypeStruct((t // tb, N_EXPERTS, LANES), F32)],
        compiler_params=_cparams(("parallel",)),
        name=f"outproj_route_l{layer}",
    )(o, w, x2, mod4, ffn_nw.reshape(1, d), wr_t, br_col)


class _MoePlan:
    def __init__(self, t, tb):
        self.tb = tb
        self.n_tiles = t // tb
        self.local_rows = -(-(TOP_K * tb + N_EXPERTS * (ROW_GROUP - 1)) // PERM_BLOCK) * PERM_BLOCK
        self.ffn_tile = min(FFN_ROW_TILE, TOP_K * tb)
        worst_rows = TOP_K * t + self.n_tiles * N_EXPERTS * (ROW_GROUP - 1)
        self.ffn_tiles = -(-worst_rows // self.ffn_tile) + N_EXPERTS
        self.sorted_rows = self.ffn_tiles * self.ffn_tile


def _moe_tables(counts, plan):
    i32 = jnp.int32
    n_pad = (counts + (ROW_GROUP - 1)) // ROW_GROUP * ROW_GROUP
    ends = jnp.cumsum(n_pad, axis=1)
    loc = ends - n_pad
    n_e = jnp.sum(n_pad, axis=0)
    n_e_pad = -(-n_e // plan.ffn_tile) * plan.ffn_tile
    region_end = jnp.cumsum(n_e_pad)
    base = region_end - n_e_pad
    goff = base[None, :] + jnp.cumsum(n_pad, axis=0) - n_pad
    n_used = region_end[-1] // plan.ffn_tile
    tile_row = jnp.minimum(jnp.arange(plan.ffn_tiles, dtype=i32), n_used - 1) * plan.ffn_tile
    tile_expert = jnp.minimum(jnp.sum(region_end[None, :] <= tile_row[:, None], axis=-1), N_EXPERTS - 1)
    flat = lambda a: a.reshape(-1).astype(i32)
    return dict(loc=loc, seg_src=flat(loc), seg_dst=flat(goff), seg_rows=flat(n_pad),
                tail_start=(base + n_e).astype(i32), tail_rows=(n_e_pad - n_e).astype(i32),
                tile_expert=tile_expert.astype(i32), n_used=n_used.reshape(1).astype(i32))


def _segment_copy(buf_ref, slot, src_row, hbm_ref, dst_row, n_rows, sem, to_hbm):
    al = lambda v: pl.multiple_of(v, ROW_GROUP)
    local = buf_ref.at[slot, pl.ds(al(src_row), al(n_rows))]
    remote = hbm_ref.at[pl.ds(al(dst_row), al(n_rows))]
    src, dst = (local, remote) if to_hbm else (remote, local)
    return pltpu.make_async_copy(src, dst, sem.at[slot])


def _tile_segments(tile, slot, buf_ref, hbm_ref, sem, src_ref, dst_ref, rows_ref, to_hbm, start):
    def seg(e, c):
        k = tile * N_EXPERTS + e
        n = rows_ref[k]

        @pl.when(n > 0)
        def _():
            cp = _segment_copy(buf_ref, slot, src_ref[k], hbm_ref, dst_ref[k], n, sem, to_hbm)
            cp.start() if start else cp.wait()
        return c

    lax.fori_loop(0, N_EXPERTS, seg, 0)


def _dispatch_kernel(src_ref, dst_ref, rows_ref, tstart_ref, trows_ref, nu_ref, h_ref, meta_ref, loc_ref,
                     sorted_ref, cmeta_ref, buf_ref, zero_ref, sem, zsem, *, plan):
    i = pl.program_id(0)
    last = pl.num_programs(0) - 1
    slot = i % 2
    tb, rows_l = plan.tb, plan.local_rows

    def tails(start):
        def tail(e, c):
            n = trows_ref[e]

            @pl.when(n > 0)
            def _():
                cp = pltpu.make_async_copy(zero_ref.at[pl.ds(0, pl.multiple_of(n, ROW_GROUP))],
                                           sorted_ref.at[pl.ds(pl.multiple_of(tstart_ref[e], ROW_GROUP),
                                                               pl.multiple_of(n, ROW_GROUP))], zsem.at[0])
                cp.start() if start else cp.wait()
            return c

        lax.fori_loop(0, N_EXPERTS, tail, 0)

    def spare_copy(j):
        row = j * plan.ffn_tile if isinstance(j, int) else pl.multiple_of(j * plan.ffn_tile, plan.ffn_tile)
        return pltpu.make_async_copy(zero_ref, sorted_ref.at[pl.ds(row, plan.ffn_tile)], zsem.at[1])

    @pl.when(i == 0)
    def _():
        zero_ref[...] = jnp.zeros_like(zero_ref)
        tails(True)
        lax.fori_loop(nu_ref[0], plan.ffn_tiles, lambda j, c: (spare_copy(j).start(), c)[1], 0)

    def wait_tile(tile, s):
        _tile_segments(tile, s, buf_ref, sorted_ref, sem, src_ref, dst_ref, rows_ref, True, False)

    @pl.when(i >= 2)
    def _():
        wait_tile(i - 2, slot)

    meta = meta_ref[...]
    eid = lax.broadcasted_iota(jnp.int32, (N_EXPERTS, tb), 0).astype(F32)
    loc_col = loc_ref[...]

    def local_row(e_row, rank_row):
        return jnp.sum(jnp.where(eid == e_row, loc_col, 0.0), axis=0, keepdims=True) + rank_row

    r1_row = local_row(meta[0:1, :], meta[4:5, :])
    r2_row = local_row(meta[1:2, :], meta[5:6, :])
    r1h, r2h = jnp.floor(r1_row * (1.0 / 256.0)), jnp.floor(r2_row * (1.0 / 256.0))
    pieces = _row_pick(LANES, [r1h, r1_row - 256.0 * r1h, r2h, r2_row - 256.0 * r2h,
                               meta[2:3, :], meta[3:4, :]]).astype(BF16)
    eye = jnp.where(lax.broadcasted_iota(jnp.int32, (tb, tb), 0) == lax.broadcasted_iota(jnp.int32, (tb, tb), 1),
                    1.0, 0.0).astype(BF16)
    cmeta_ref[...] = _dot_nt(eye, pieces)
    h = h_ref[...]
    blk = PERM_BLOCK
    for rb in range(rows_l // blk):
        rid = (lax.broadcasted_iota(jnp.int32, (blk, tb), 0) + rb * blk).astype(F32)
        perm = jnp.where(rid == r1_row, 1.0, jnp.where(rid == r2_row, 1.0, 0.0)).astype(BF16)
        buf_ref[slot, rb * blk:(rb + 1) * blk, :] = jnp.dot(perm, h, preferred_element_type=F32).astype(BF16)

    _tile_segments(i, slot, buf_ref, sorted_ref, sem, src_ref, dst_ref, rows_ref, True, True)

    @pl.when(i == last)
    def _():
        @pl.when(i >= 1)
        def _():
            wait_tile(i - 1, 1 - slot)
        wait_tile(i, slot)
        tails(False)
        lax.fori_loop(nu_ref[0], plan.ffn_tiles, lambda j, c: (spare_copy(0).wait(), c)[1], 0)


def _dispatch(h2, meta, tabs, plan):
    t, d = h2.shape
    loc_col = tabs["loc"].astype(F32)[:, :, None]
    grid_spec = pltpu.PrefetchScalarGridSpec(
        num_scalar_prefetch=6,
        grid=(plan.n_tiles,),
        in_specs=[pl.BlockSpec((plan.tb, d), lambda i, *_: (i, 0)),
                  pl.BlockSpec((None, SUBLANES, plan.tb), lambda i, *_: (i, 0, 0)),
                  pl.BlockSpec((None, N_EXPERTS, 1), lambda i, *_: (i, 0, 0))],
        out_specs=[pl.BlockSpec(memory_space=pl.ANY),
                   pl.BlockSpec((plan.tb, LANES), lambda i, *_: (i, 0))],
        scratch_shapes=[pltpu.VMEM((2, plan.local_rows, d), BF16),
                        pltpu.VMEM((plan.ffn_tile, d), BF16),
                        pltpu.SemaphoreType.DMA((2,)),
                        pltpu.SemaphoreType.DMA((2,))])
    return pl.pallas_call(
        functools.partial(_dispatch_kernel, plan=plan),
        grid_spec=grid_spec,
        out_shape=[jax.ShapeDtypeStruct((plan.sorted_rows, d), BF16),
                   jax.ShapeDtypeStruct((t, LANES), F32)],
        compiler_params=_cparams(("arbitrary",)),
        name="moe_dispatch",
    )(tabs["seg_src"], tabs["seg_dst"], tabs["seg_rows"], tabs["tail_start"], tabs["tail_rows"], tabs["n_used"],
      h2, meta, loc_col)


def _ffn_kernel(te_ref, nu_ref, x_ref, wg_ref, wu_ref, wd_ref, y_ref, wg_s, wu_s, wd_s):
    j = pl.program_id(0)
    used = j < nu_ref[0]
    new_expert = jnp.logical_or(j == 0, te_ref[j] != te_ref[jnp.maximum(j - 1, 0)])

    @pl.when(jnp.logical_and(used, new_expert))
    def _():
        wg_s[...] = wg_ref[0].astype(BF16)
        wu_s[...] = wu_ref[0].astype(BF16)
        wd_s[...] = wd_ref[0].astype(BF16)

    @pl.when(used)
    def _():
        x = x_ref[...]
        g = jnp.dot(x, wg_s[...], preferred_element_type=F32)
        u = jnp.dot(x, wu_s[...], preferred_element_type=F32)
        y_ref[...] = jnp.dot((_silu(g) * u).astype(BF16), wd_s[...], preferred_element_type=F32).astype(BF16)

    @pl.when(jnp.logical_not(used))
    def _():
        y_ref[...] = jnp.zeros_like(y_ref)


def _ffn(xs, wg, wu, wd, layer, tabs, plan):
    d, de = wg.shape[2], wg.shape[3]
    tm = plan.ffn_tile
    row_blk = lambda j, te, nu: (jnp.minimum(j, nu[0] - 1), 0)
    grid_spec = pltpu.PrefetchScalarGridSpec(
        num_scalar_prefetch=2,
        grid=(plan.ffn_tiles,),
        in_specs=[pl.BlockSpec((tm, d), row_blk),
                  pl.BlockSpec((None, 1, d, de), lambda j, te, nu: (layer, te[j], 0, 0)),
                  pl.BlockSpec((None, 1, d, de), lambda j, te, nu: (layer, te[j], 0, 0)),
                  pl.BlockSpec((None, 1, de, d), lambda j, te, nu: (layer, te[j], 0, 0))],
        out_specs=pl.BlockSpec((tm, d), lambda j, te, nu: (j, 0)),
        scratch_shapes=[pltpu.VMEM((d, de), BF16), pltpu.VMEM((d, de), BF16), pltpu.VMEM((de, d), BF16)])
    return pl.pallas_call(
        _ffn_kernel,
        grid_spec=grid_spec,
        out_shape=jax.ShapeDtypeStruct(xs.shape, BF16),
        compiler_params=_cparams(("arbitrary",)),
        name="moe_ffn",
    )(tabs["tile_expert"], tabs["n_used"], xs, wg, wu, wd)


def _combine_kernel(src_ref, dst_ref, rows_ref, ys_ref, cmeta_ref, x_ref, mod_ref, fw_ref, o_ref, buf_ref, sem,
                    *, plan, final):
    i = pl.program_id(0)
    n_tiles = pl.num_programs(0)
    slot = i % 2

    def fetch(tile, s, start):
        _tile_segments(tile, s, buf_ref, ys_ref, sem, src_ref, dst_ref, rows_ref, False, start)

    @pl.when(i == 0)
    def _():
        buf_ref[...] = jnp.zeros_like(buf_ref)
        fetch(0, 0, True)

    @pl.when(i + 1 < n_tiles)
    def _():
        fetch(i + 1, 1 - slot, True)

    fetch(i, slot, False)

    cm = cmeta_ref[...]
    r1 = cm[:, 0:1] * 256.0 + cm[:, 1:2]
    r2 = cm[:, 2:3] * 256.0 + cm[:, 3:4]
    rid = lax.broadcasted_iota(jnp.int32, (plan.tb, plan.local_rows), 1).astype(F32)
    comb = (jnp.where(rid == r1, cm[:, 4:5], 0.0) + jnp.where(rid == r2, cm[:, 5:6], 0.0)).astype(BF16)
    moe = jnp.dot(comb, buf_ref[slot], preferred_element_type=F32)
    xn = x_ref[...] + mod_ref[5:6, :] * moe
    if final:
        ms = jnp.mean(xn * xn, axis=-1, keepdims=True)
        xn = xn * lax.rsqrt(ms + EPS) * fw_ref[...]
    o_ref[...] = xn


def _combine(ys, cmeta, x1, mod4, layer, seq, tabs, plan, final_w, final):
    t, d = x1.shape
    per_b = seq // plan.tb
    grid_spec = pltpu.PrefetchScalarGridSpec(
        num_scalar_prefetch=3,
        grid=(plan.n_tiles,),
        in_specs=[pl.BlockSpec(memory_space=pl.ANY),
                  pl.BlockSpec((plan.tb, LANES), lambda i, *_: (i, 0)),
                  pl.BlockSpec((plan.tb, d), lambda i, *_: (i, 0)),
                  pl.BlockSpec((None, None, 6, d), lambda i, *_: (layer, i // per_b, 0, 0)),
                  pl.BlockSpec((1, d), lambda i, *_: (0, 0))],
        out_specs=pl.BlockSpec((plan.tb, d), lambda i, *_: (i, 0)),
        scratch_shapes=[pltpu.VMEM((2, plan.local_rows, d), BF16),
                        pltpu.SemaphoreType.DMA((2,))])
    return pl.pallas_call(
        functools.partial(_combine_kernel, plan=plan, final=final),
        grid_spec=grid_spec,
        out_shape=jax.ShapeDtypeStruct((t, d), F32),
        compiler_params=_cparams(("arbitrary",)),
        name="moe_combine",
    )(tabs["seg_src"], tabs["seg_dst"], tabs["seg_rows"], ys, cmeta, x1, mod4, final_w.reshape(1, d))


def kernel(x, c, w_ada, b_ada, norm_mix_w, norm_ffn_w, w_in_even, hg_lb_logits, hg_norm_w, ret_norm_w,
           w_out_even, w_in_odd, w_gla_a2, b_gla_a2, gla_norm_w, w_out_odd, w_router, b_router,
           w_gate, w_up, w_down, final_norm_w):
    bsz, seq, d = x.shape
    depth = w_ada.shape[0]
    x2 = x.reshape(bsz * seq, d)
    mod4 = _adaln_mod(c, w_ada, b_ada).reshape(depth, bsz, 6, d)
    wr_t = w_router.T.astype(BF16)
    br_col = b_router.reshape(N_EXPERTS, 1)
    plan = _MoePlan(bsz * seq, min(MOE_TOKEN_TILE, seq))
    for l in range(depth):
        j = l // 2
        if l % 2 == 0:
            proj, hf = _inproj(x2, mod4, l, seq, norm_mix_w[l], w_in_even[j].astype(BF16),
                               _inproj_even_kernel, w_in_even.shape[-1], [], [])
            o = _even_mixers(proj, hf, hg_lb_logits, hg_norm_w[j], ret_norm_w[j], bsz, seq, l)
            w_out = w_out_even[j].astype(BF16)
        else:
            n_main = w_in_odd.shape[-1] - GLA_RANK
            wi = w_in_odd[j]
            wa = jnp.pad(wi[:, n_main:], ((0, 0), (0, LANES - GLA_RANK))).astype(BF16)
            wa2 = jnp.pad(w_gla_a2[j], ((0, LANES - GLA_RANK), (0, 0))).astype(BF16)
            full = lambda a: pl.BlockSpec(a.shape, lambda i: (0,) * a.ndim)
            ba2 = b_gla_a2[j].reshape(1, -1)
            proj, log_a = _inproj(x2, mod4, l, seq, norm_mix_w[l], wi[:, :n_main].astype(BF16),
                                  _inproj_odd_kernel, n_main, [full(wa), full(wa2), full(ba2)], [wa, wa2, ba2])
            o = _odd_mixer(proj, log_a, gla_norm_w[j], bsz, seq, l)
            w_out = w_out_odd[j].astype(BF16)
        x1, h2, meta, counts = _outproj_route(o, w_out, x2, mod4, l, seq, plan.tb, norm_ffn_w[l], wr_t, br_col)
        tabs = _moe_tables(counts[:, :, 0].astype(jnp.int32), plan)
        xs, cmeta = _dispatch(h2, meta, tabs, plan)
        ys = _ffn(xs, w_gate, w_up, w_down, l, tabs, plan)
        x2 = _combine(ys, cmeta, x1, mod4, l, seq, tabs, plan, final_norm_w, final=(l == depth - 1))
    return x2.reshape(bsz, seq, d)
```

```python
import functools

import jax
import jax.numpy as jnp
from jax import lax
from jax.experimental import pallas as pl
from jax.experimental.pallas import tpu as pltpu

F32 = jnp.float32
BF16 = jnp.bfloat16
HIGHEST = lax.Precision.HIGHEST

EPS = 1e-6
CHUNK = 64
RET_CHUNK = 256
HEAD_DIM = 128
N_HEADS = 4
N_EXPERTS = 16
N_GROUPS = 4
EXPERTS_PER_GROUP = N_EXPERTS // N_GROUPS
GLA_RANK = 16
GLA_GATE_NORM = 16.0
ROPE_BASE = 10000.0
LANES = 128
VMEM_LIMIT = 52 * 1024 * 1024
SUBLANES = 8
INPROJ_ROW_TILE = 512
MIX_SEQ_TILE = 512
TOP_K = 2
ROW_GROUP = 16
MOE_TOKEN_TILE = 512
FFN_ROW_TILE = 1024
PERM_BLOCK = 256


def _cparams(sem):
    return pltpu.CompilerParams(dimension_semantics=sem, vmem_limit_bytes=VMEM_LIMIT)


def _sigmoid(z):
    return 0.5 * jnp.tanh(0.5 * z) + 0.5


def _silu(z):
    return z * _sigmoid(z)


def _mod_kernel(c_ref, w_ref, b_ref, o_ref):
    c = c_ref[...]
    o_ref[0] = jnp.dot(_silu(c), w_ref[0], preferred_element_type=F32, precision=HIGHEST) + b_ref[0]


def _adaln_mod(c, w_ada, b_ada):
    depth, d, n = w_ada.shape
    bsz = c.shape[0]
    tn = 1536
    return pl.pallas_call(
        _mod_kernel,
        grid=(depth, n // tn),
        in_specs=[pl.BlockSpec((bsz, d), lambda l, j: (0, 0)),
                  pl.BlockSpec((1, d, tn), lambda l, j: (l, 0, j)),
                  pl.BlockSpec((1, 1, tn), lambda l, j: (l, 0, j))],
        out_specs=pl.BlockSpec((1, bsz, tn), lambda l, j: (l, 0, j)),
        out_shape=jax.ShapeDtypeStruct((depth, bsz, n), F32),
        compiler_params=_cparams(("arbitrary", "arbitrary")),
        name="adaln_mod",
    )(c, w_ada, b_ada.reshape(depth, 1, n))


def _norm_mod(x, nw, mod_ref, shift_row, scale_row):
    ms = jnp.mean(x * x, axis=-1, keepdims=True)
    return x * lax.rsqrt(ms + EPS) * (nw * (1.0 + mod_ref[scale_row:scale_row + 1, :])) + mod_ref[shift_row:shift_row + 1, :]


def _inproj_even_kernel(x_ref, mod_ref, nw_ref, w_ref, o_ref, f_ref):
    h = _norm_mod(x_ref[...], nw_ref[...], mod_ref, 0, 1)
    res = jnp.dot(h.astype(BF16), w_ref[...], preferred_element_type=F32)
    o_ref[...] = res.astype(BF16)
    f_ref[...] = res[:, 512:1024]


def _log_sigmoid(z):
    return jnp.minimum(z, 0.0) - jnp.log1p(jnp.exp(-jnp.abs(z)))


def _inproj_odd_kernel(x_ref, mod_ref, nw_ref, w_ref, wa_ref, wa2_ref, ba2_ref, o_ref, la_ref):
    h = _norm_mod(x_ref[...], nw_ref[...], mod_ref, 0, 1).astype(BF16)
    o_ref[...] = jnp.dot(h, w_ref[...], preferred_element_type=F32).astype(BF16)
    ga = jnp.dot(h, wa_ref[...], preferred_element_type=F32)
    z = jnp.dot(ga.astype(BF16), wa2_ref[...], preferred_element_type=F32) + ba2_ref[...]
    la_ref[...] = _log_sigmoid(z) * (1.0 / GLA_GATE_NORM)


def _inproj(x2, mod4, layer, seq, nw, weights, kernel_fn, n_main, extra_specs, extra_args):
    t, d = x2.shape
    tm = min(INPROJ_ROW_TILE, seq)
    per_b = seq // tm
    return pl.pallas_call(
        kernel_fn,
        grid=(t // tm,),
        in_specs=[pl.BlockSpec((tm, d), lambda i: (i, 0)),
                  pl.BlockSpec((None, None, 6, d), lambda i: (layer, i // per_b, 0, 0)),
                  pl.BlockSpec((1, d), lambda i: (0, 0)),
                  pl.BlockSpec((d, n_main), lambda i: (0, 0))] + extra_specs,
        out_specs=[pl.BlockSpec((tm, n_main), lambda i: (i, 0)),
                   pl.BlockSpec((tm, 512), lambda i: (i, 0))],
        out_shape=[jax.ShapeDtypeStruct((t, n_main), BF16),
                   jax.ShapeDtypeStruct((t, 512), F32)],
        compiler_params=_cparams(("parallel",)),
        name=f"inproj_l{layer}",
    )(x2, mod4, nw.reshape(1, d), weights, *extra_args)


def _causal_mask():
    r = lax.broadcasted_iota(jnp.int32, (CHUNK, CHUNK), 0)
    c = lax.broadcasted_iota(jnp.int32, (CHUNK, CHUNK), 1)
    return r >= c


def _dot_nt(a, b):
    return lax.dot_general(a, b, (((1,), (1,)), ((), ())), preferred_element_type=F32)


def _dot_tn(a, b):
    return lax.dot_general(a, b, (((0,), (0,)), ((), ())), preferred_element_type=F32)


def _head_out(o, g, nw, n_norm):
    ms = jnp.sum(o * o, axis=-1, keepdims=True) * (1.0 / n_norm)
    return o * lax.rsqrt(ms + EPS) * nw * _silu(g)


def _cumsum_rows(lg):
    tril = jnp.where(_causal_mask(), 1.0, 0.0).astype(BF16)
    hi = lg.astype(BF16)
    lo = (lg - hi.astype(F32)).astype(BF16)
    return jnp.dot(tril, hi, preferred_element_type=F32) + jnp.dot(tril, lo, preferred_element_type=F32)


def _decay_heads(q, k, v, b, st_ref, st_base, dv):
    b_mid = b[CHUNK // 2 - 1:CHUNK // 2, :]
    b_last = b[CHUNK - 1:CHUNK, :]
    qe = q * jnp.exp(b - b_mid)
    ke = k * jnp.exp(b_mid - b)
    qs = qe.astype(BF16)
    ks = ke.astype(BF16)
    qi = (qe * jnp.exp(b_mid)).astype(BF16)
    kd = (ke * jnp.exp(b_last - b_mid)).astype(BF16)
    dec = jnp.exp(b_last)
    mask = _causal_mask()
    outs = []
    for h in range(N_HEADS):
        kl = slice(h * HEAD_DIM, (h + 1) * HEAD_DIM)
        vh = v[:, h * dv:(h + 1) * dv]
        scores = jnp.where(mask, _dot_nt(qs[:, kl], ks[:, kl]), 0.0)
        o = jnp.dot(scores.astype(BF16), vh, preferred_element_type=F32)
        st = st_ref[st_base + h]
        o = o + _dot_nt(qi[:, kl], st.astype(BF16))
        st_ref[st_base + h] = dec[:, kl] * st + _dot_tn(vh, kd[:, kl])
        outs.append(o)
    return outs


def _retention_log_decay():
    headf = (lax.broadcasted_iota(jnp.int32, (1, N_HEADS * HEAD_DIM), 1) // HEAD_DIM).astype(F32)
    return jnp.log1p(-jnp.exp2(-5.0 - headf))


def _even_mixer_kernel(proj_ref, hf_ref, lbl_ref, hgnw_ref, retnw_ref, cos_ref, sin_ref, o_ref,
                       st_ref, eq_ref, ek_ref, dm_ref, *, layer, n_chunks, ret_chunk):
    hd, w = HEAD_DIM, N_HEADS * HEAD_DIM
    lg = _retention_log_decay()

    @pl.when(jnp.logical_and(pl.program_id(0) == 0, pl.program_id(1) == 0))
    def _():
        pos = lax.broadcasted_iota(jnp.int32, (ret_chunk, w), 0).astype(F32)
        eq_ref[...] = jnp.exp((pos + 1.0) * lg)
        ek_ref[...] = jnp.exp((ret_chunk - 1.0 - pos) * lg)
        r = lax.broadcasted_iota(jnp.int32, (ret_chunk, ret_chunk), 0)
        cc = lax.broadcasted_iota(jnp.int32, (ret_chunk, ret_chunk), 1)
        dt = (r - cc).astype(F32)
        for h in range(N_HEADS):
            dm_ref[h] = jnp.where(r >= cc, jnp.exp(dt * lg[:, h * hd:h * hd + 1]), 0.0)

    @pl.when(pl.program_id(1) == 0)
    def _():
        st_ref[...] = jnp.zeros_like(st_ref)

    lbl = lbl_ref[...]
    e = jnp.exp(lbl - jnp.max(lbl, axis=0, keepdims=True))
    p = e / jnp.sum(e, axis=0, keepdims=True)
    lb = jnp.sum(p[:layer + 1], axis=0, keepdims=True)
    hgnw = hgnw_ref[...]
    retnw = retnw_ref[...]
    e_c = jnp.exp(ret_chunk * lg)

    def rot(xx, cs, sn):
        return xx * cs + pltpu.roll(xx, hd // 2, 1) * sn

    for c in range(n_chunks):
        sl = pl.ds(c * CHUNK, CHUNK)
        f = lb + (1.0 - lb) * _sigmoid(hf_ref[sl, :])
        b = _cumsum_rows(jnp.log(f))
        q = _sigmoid(proj_ref[sl, 0:w].astype(F32))
        outs = _decay_heads(q, 1.0 - f, proj_ref[sl, 2 * w:3 * w], b, st_ref, 0, hd)
        for h, o in enumerate(outs):
            cl = slice(h * hd, (h + 1) * hd)
            g = proj_ref[sl, 3 * w + h * hd:3 * w + (h + 1) * hd].astype(F32)
            o_ref[sl, cl] = _head_out(o, g, hgnw[:, cl], hd).astype(BF16)

    for c in range(n_chunks * CHUNK // ret_chunk):
        sl = pl.ds(c * ret_chunk, ret_chunk)
        cs = cos_ref[sl, :]
        sn = sin_ref[sl, :]
        for h in range(N_HEADS):
            cl = slice(h * hd, (h + 1) * hd)
            qh = rot(proj_ref[sl, 4 * w + h * hd:4 * w + (h + 1) * hd].astype(F32), cs, sn)
            kh = rot(proj_ref[sl, 5 * w + h * hd:5 * w + (h + 1) * hd].astype(F32), cs, sn) * (hd ** -0.5)
            vh = proj_ref[sl, 6 * w + h * hd:6 * w + (h + 1) * hd]
            scores = _dot_nt(qh.astype(BF16), kh.astype(BF16)) * dm_ref[h]
            o = jnp.dot(scores.astype(BF16), vh, preferred_element_type=F32)
            st = st_ref[N_HEADS + h]
            o = o + _dot_nt((qh * eq_ref[:, cl]).astype(BF16), st.astype(BF16))
            st_ref[N_HEADS + h] = e_c[:, cl] * st + _dot_tn(vh, (kh * ek_ref[:, cl]).astype(BF16))
            g = proj_ref[sl, 7 * w + h * hd:7 * w + (h + 1) * hd].astype(F32)
            o_ref[sl, w + h * hd:w + (h + 1) * hd] = _head_out(o, g, retnw[:, cl], hd).astype(BF16)


def _gla_mixer_kernel(proj_ref, la_ref, nw_ref, o_ref, st_ref, *, n_chunks):
    hd, w = HEAD_DIM, N_HEADS * HEAD_DIM
    dv = 2 * hd

    @pl.when(pl.program_id(1) == 0)
    def _():
        st_ref[...] = jnp.zeros_like(st_ref)

    nw = nw_ref[...]
    for c in range(n_chunks):
        sl = pl.ds(c * CHUNK, CHUNK)
        b = _cumsum_rows(la_ref[sl, :])
        q = proj_ref[sl, 0:w].astype(F32) * (hd ** -0.5)
        k = proj_ref[sl, w:2 * w].astype(F32)
        outs = _decay_heads(q, k, proj_ref[sl, 2 * w:2 * w + N_HEADS * dv], b, st_ref, 0, dv)
        for h, o in enumerate(outs):
            cl = slice(h * dv, (h + 1) * dv)
            g = proj_ref[sl, 4 * w + h * dv:4 * w + (h + 1) * dv].astype(F32)
            o_ref[sl, cl] = _head_out(o, g, nw[:, cl], dv).astype(BF16)


def _mixer_call(kernel_fn, bsz, seq, ts, in_specs, scratch_shapes, name, args):
    d_out = N_HEADS * 2 * HEAD_DIM
    return pl.pallas_call(
        kernel_fn,
        grid=(bsz, seq // ts),
        in_specs=in_specs,
        out_specs=pl.BlockSpec((ts, d_out), lambda b, s: (b * (seq // ts) + s, 0)),
        out_shape=jax.ShapeDtypeStruct((bsz * seq, d_out), BF16),
        scratch_shapes=scratch_shapes,
        compiler_params=_cparams(("arbitrary", "arbitrary")),
        name=name,
    )(*args)


def _even_mixers(proj, hf, lb_logits, hg_nw, ret_nw, bsz, seq, layer):
    ts = min(MIX_SEQ_TILE, seq)
    rc = min(RET_CHUNK, ts)
    per_b = seq // ts
    hd, w = HEAD_DIM, N_HEADS * HEAD_DIM
    rows = lambda width: pl.BlockSpec((ts, width), lambda b, s: (b * per_b + s, 0))
    full = lambda a: pl.BlockSpec(a.shape, lambda b, s: (0,) * a.ndim)
    inv = ROPE_BASE ** (-jnp.arange(0, hd, 2, dtype=F32) / hd)
    ang = jnp.arange(seq, dtype=F32)[:, None] * inv[None, :]
    cos2 = jnp.concatenate([jnp.cos(ang), jnp.cos(ang)], axis=-1)
    sin2 = jnp.concatenate([-jnp.sin(ang), jnp.sin(ang)], axis=-1)
    tab = pl.BlockSpec((ts, hd), lambda b, s: (s, 0))
    hg_nw, ret_nw = hg_nw.reshape(1, -1), ret_nw.reshape(1, -1)
    scratch = [pltpu.VMEM((2 * N_HEADS, hd, hd), F32), pltpu.VMEM((rc, w), F32), pltpu.VMEM((rc, w), F32),
               pltpu.VMEM((N_HEADS, rc, rc), F32)]
    return _mixer_call(
        functools.partial(_even_mixer_kernel, layer=layer, n_chunks=ts // CHUNK, ret_chunk=rc), bsz, seq, ts,
        [rows(proj.shape[1]), rows(hf.shape[1]), full(lb_logits), full(hg_nw), full(ret_nw), tab, tab],
        scratch, f"even_mixer_l{layer}",
        (proj, hf, lb_logits, hg_nw, ret_nw, cos2, sin2))


def _odd_mixer(proj, log_a, gla_nw, bsz, seq, layer):
    ts = min(MIX_SEQ_TILE, seq)
    per_b = seq // ts
    rows = lambda width: pl.BlockSpec((ts, width), lambda b, s: (b * per_b + s, 0))
    gla_nw = gla_nw.reshape(1, -1)
    return _mixer_call(
        functools.partial(_gla_mixer_kernel, n_chunks=ts // CHUNK), bsz, seq, ts,
        [rows(proj.shape[1]), rows(log_a.shape[1]), pl.BlockSpec(gla_nw.shape, lambda b, s: (0, 0))],
        [pltpu.VMEM((N_HEADS, 2 * HEAD_DIM, HEAD_DIM), F32)], f"gla_mixer_l{layer}",
        (proj, log_a, gla_nw))


def _row_pick(n_rows, rows):
    rid = lax.broadcasted_iota(jnp.int32, (n_rows, rows[0].shape[1]), 0)
    out = jnp.zeros(rid.shape, F32)
    for j, row in reversed(list(enumerate(rows))):
        out = jnp.where(rid == j, row, out)
    return out


def _route(h2, wr_t, br):
    neg = -jnp.inf
    logits = _dot_nt(wr_t, h2)
    tm = logits.shape[1]
    eid = lax.broadcasted_iota(jnp.int32, logits.shape, 0)
    ex = jnp.exp(logits - jnp.max(logits, axis=0, keepdims=True))
    probs = ex / jnp.sum(ex, axis=0, keepdims=True)
    sel = probs + br
    grp = eid // EXPERTS_PER_GROUP
    gs = [jnp.max(jnp.where(grp == g, sel, neg), axis=0, keepdims=True) for g in range(N_GROUPS)]
    best = functools.reduce(jnp.maximum, gs)
    gidx = jnp.full(best.shape, N_GROUPS - 1, jnp.int32)
    for g in range(N_GROUPS - 2, -1, -1):
        gidx = jnp.where(gs[g] == best, g, gidx)
    masked = jnp.where(grp == gidx, sel, neg)
    m1 = jnp.max(masked, axis=0, keepdims=True)
    i1 = jnp.min(jnp.where(masked == m1, eid, N_EXPERTS), axis=0, keepdims=True)
    masked2 = jnp.where(eid == i1, neg, masked)
    m2 = jnp.max(masked2, axis=0, keepdims=True)
    i2 = jnp.min(jnp.where(masked2 == m2, eid, N_EXPERTS), axis=0, keepdims=True)
    p1 = jnp.sum(jnp.where(eid == i1, probs, 0.0), axis=0, keepdims=True)
    p2 = jnp.sum(jnp.where(eid == i2, probs, 0.0), axis=0, keepdims=True)
    den = p1 + p2
    chosen = jnp.where(eid == i1, 1.0, jnp.where(eid == i2, 1.0, 0.0))
    r = lax.broadcasted_iota(jnp.int32, (tm, tm), 0)
    c = lax.broadcasted_iota(jnp.int32, (tm, tm), 1)
    earlier = jnp.where(r < c, 1.0, 0.0).astype(BF16)
    rank = jnp.dot(chosen.astype(BF16), earlier, preferred_element_type=F32)
    rank1 = jnp.sum(jnp.where(eid == i1, rank, 0.0), axis=0, keepdims=True)
    rank2 = jnp.sum(jnp.where(eid == i2, rank, 0.0), axis=0, keepdims=True)
    meta = _row_pick(SUBLANES, [i1.astype(F32), i2.astype(F32), p1 / den, p2 / den, rank1, rank2])
    return meta, jnp.sum(chosen, axis=1, keepdims=True)


def _outproj_kernel(o_ref, w_ref, x_ref, mod_ref, nw_ref, wr_ref, br_ref, x1_ref, h2_ref, meta_ref, cnt_ref):
    y = jnp.dot(o_ref[...], w_ref[...], preferred_element_type=F32)
    x1 = x_ref[...] + mod_ref[2:3, :] * y
    x1_ref[...] = x1
    h2 = _norm_mod(x1, nw_ref[...], mod_ref, 3, 4).astype(BF16)
    h2_ref[...] = h2
    meta, counts = _route(h2, wr_ref[...], br_ref[...])
    meta_ref[...] = meta
    cnt_ref[...] = jnp.broadcast_to(counts, cnt_ref.shape)


def _outproj_route(o, w, x2, mod4, layer, seq, tb, ffn_nw, wr_t, br_col):
    t, d = x2.shape
    per_b = seq // tb
    full = lambda a: pl.BlockSpec(a.shape, lambda i: (0,) * a.ndim)
    rows = lambda width: pl.BlockSpec((tb, width), lambda i: (i, 0))
    return pl.pallas_call(
        _outproj_kernel,
        grid=(t // tb,),
        in_specs=[rows(o.shape[1]), full(w), rows(d),
                  pl.BlockSpec((None, None, 6, d), lambda i: (layer, i // per_b, 0, 0)),
                  pl.BlockSpec((1, d), lambda i: (0, 0)), full(wr_t), full(br_col)],
        out_specs=[rows(d), rows(d),
                   pl.BlockSpec((None, SUBLANES, tb), lambda i: (i, 0, 0)),
                   pl.BlockSpec((None, N_EXPERTS, LANES), lambda i: (i, 0, 0))],
        out_shape=[jax.ShapeDtypeStruct((t, d), F32),
                   jax.ShapeDtypeStruct((t, d), BF16),
                   jax.ShapeDtypeStruct((t // tb, SUBLANES, tb), F32),
                   jax.ShapeDtypeStruct((t // tb, N_EXPERTS, LANES), F32)],
        compiler_params=_cparams(("parallel",)),
        name=f"outproj_route_l{layer}",
    )(o, w, x2, mod4, ffn_nw.reshape(1, d), wr_t, br_col)


class _MoePlan:
    def __init__(self, t, tb):
        self.tb = tb
        self.n_tiles = t // tb
        self.local_rows = -(-(TOP_K * tb + N_EXPERTS * (ROW_GROUP - 1)) // PERM_BLOCK) * PERM_BLOCK
        self.ffn_tile = min(FFN_ROW_TILE, TOP_K * tb)
        worst_rows = TOP_K * t + self.n_tiles * N_EXPERTS * (ROW_GROUP - 1)
        self.ffn_tiles = -(-worst_rows // self.ffn_tile) + N_EXPERTS
        self.sorted_rows = self.ffn_tiles * self.ffn_tile


def _moe_tables(counts, plan):
    i32 = jnp.int32
    n_pad = (counts + (ROW_GROUP - 1)) // ROW_GROUP * ROW_GROUP
    ends = jnp.cumsum(n_pad, axis=1)
    loc = ends - n_pad
    n_e = jnp.sum(n_pad, axis=0)
    n_e_pad = -(-n_e // plan.ffn_tile) * plan.ffn_tile
    region_end = jnp.cumsum(n_e_pad)
    base = region_end - n_e_pad
    goff = base[None, :] + jnp.cumsum(n_pad, axis=0) - n_pad
    n_used = region_end[-1] // plan.ffn_tile
    tile_row = jnp.minimum(jnp.arange(plan.ffn_tiles, dtype=i32), n_used - 1) * plan.ffn_tile
    tile_expert = jnp.minimum(jnp.sum(region_end[None, :] <= tile_row[:, None], axis=-1), N_EXPERTS - 1)
    flat = lambda a: a.reshape(-1).astype(i32)
    return dict(loc=loc, seg_src=flat(loc), seg_dst=flat(goff), seg_rows=flat(n_pad),
                tail_start=(base + n_e).astype(i32), tail_rows=(n_e_pad - n_e).astype(i32),
                tile_expert=tile_expert.astype(i32), n_used=n_used.reshape(1).astype(i32))


def _segment_copy(buf_ref, slot, src_row, hbm_ref, dst_row, n_rows, sem, to_hbm):
    al = lambda v: pl.multiple_of(v, ROW_GROUP)
    local = buf_ref.at[slot, pl.ds(al(src_row), al(n_rows))]
    remote = hbm_ref.at[pl.ds(al(dst_row), al(n_rows))]
    src, dst = (local, remote) if to_hbm else (remote, local)
    return pltpu.make_async_copy(src, dst, sem.at[slot])


def _tile_segments(tile, slot, buf_ref, hbm_ref, sem, src_ref, dst_ref, rows_ref, to_hbm, start):
    def seg(e, c):
        k = tile * N_EXPERTS + e
        n = rows_ref[k]

        @pl.when(n > 0)
        def _():
            cp = _segment_copy(buf_ref, slot, src_ref[k], hbm_ref, dst_ref[k], n, sem, to_hbm)
            cp.start() if start else cp.wait()
        return c

    lax.fori_loop(0, N_EXPERTS, seg, 0)


def _dispatch_kernel(src_ref, dst_ref, rows_ref, tstart_ref, trows_ref, nu_ref, h_ref, meta_ref, loc_ref,
                     sorted_ref, cmeta_ref, buf_ref, zero_ref, sem, zsem, *, plan):
    i = pl.program_id(0)
    last = pl.num_programs(0) - 1
    slot = i % 2
    tb, rows_l = plan.tb, plan.local_rows

    def tails(start):
        def tail(e, c):
            n = trows_ref[e]

            @pl.when(n > 0)
            def _():
                cp = pltpu.make_async_copy(zero_ref.at[pl.ds(0, pl.multiple_of(n, ROW_GROUP))],
                                           sorted_ref.at[pl.ds(pl.multiple_of(tstart_ref[e], ROW_GROUP),
                                                               pl.multiple_of(n, ROW_GROUP))], zsem.at[0])
                cp.start() if start else cp.wait()
            return c

        lax.fori_loop(0, N_EXPERTS, tail, 0)

    def spare_copy(j):
        row = j * plan.ffn_tile if isinstance(j, int) else pl.multiple_of(j * plan.ffn_tile, plan.ffn_tile)
        return pltpu.make_async_copy(zero_ref, sorted_ref.at[pl.ds(row, plan.ffn_tile)], zsem.at[1])

    @pl.when(i == 0)
    def _():
        zero_ref[...] = jnp.zeros_like(zero_ref)
        tails(True)
        lax.fori_loop(nu_ref[0], plan.ffn_tiles, lambda j, c: (spare_copy(j).start(), c)[1], 0)

    def wait_tile(tile, s):
        _tile_segments(tile, s, buf_ref, sorted_ref, sem, src_ref, dst_ref, rows_ref, True, False)

    @pl.when(i >= 2)
    def _():
        wait_tile(i - 2, slot)

    meta = meta_ref[...]
    eid = lax.broadcasted_iota(jnp.int32, (N_EXPERTS, tb), 0).astype(F32)
    loc_col = loc_ref[...]

    def local_row(e_row, rank_row):
        return jnp.sum(jnp.where(eid == e_row, loc_col, 0.0), axis=0, keepdims=True) + rank_row

    r1_row = local_row(meta[0:1, :], meta[4:5, :])
    r2_row = local_row(meta[1:2, :], meta[5:6, :])
    r1h, r2h = jnp.floor(r1_row * (1.0 / 256.0)), jnp.floor(r2_row * (1.0 / 256.0))
    pieces = _row_pick(LANES, [r1h, r1_row - 256.0 * r1h, r2h, r2_row - 256.0 * r2h,
                               meta[2:3, :], meta[3:4, :]]).astype(BF16)
    eye = jnp.where(lax.broadcasted_iota(jnp.int32, (tb, tb), 0) == lax.broadcasted_iota(jnp.int32, (tb, tb), 1),
                    1.0, 0.0).astype(BF16)
    cmeta_ref[...] = _dot_nt(eye, pieces)
    h = h_ref[...]
    blk = PERM_BLOCK
    for rb in range(rows_l // blk):
        rid = (lax.broadcasted_iota(jnp.int32, (blk, tb), 0) + rb * blk).astype(F32)
        perm = jnp.where(rid == r1_row, 1.0, jnp.where(rid == r2_row, 1.0, 0.0)).astype(BF16)
        buf_ref[slot, rb * blk:(rb + 1) * blk, :] = jnp.dot(perm, h, preferred_element_type=F32).astype(BF16)

    _tile_segments(i, slot, buf_ref, sorted_ref, sem, src_ref, dst_ref, rows_ref, True, True)

    @pl.when(i == last)
    def _():
        @pl.when(i >= 1)
        def _():
            wait_tile(i - 1, 1 - slot)
        wait_tile(i, slot)
        tails(False)
        lax.fori_loop(nu_ref[0], plan.ffn_tiles, lambda j, c: (spare_copy(0).wait(), c)[1], 0)


def _dispatch(h2, meta, tabs, plan):
    t, d = h2.shape
    loc_col = tabs["loc"].astype(F32)[:, :, None]
    grid_spec = pltpu.PrefetchScalarGridSpec(
        num_scalar_prefetch=6,
        grid=(plan.n_tiles,),
        in_specs=[pl.BlockSpec((plan.tb, d), lambda i, *_: (i, 0)),
                  pl.BlockSpec((None, SUBLANES, plan.tb), lambda i, *_: (i, 0, 0)),
                  pl.BlockSpec((None, N_EXPERTS, 1), lambda i, *_: (i, 0, 0))],
        out_specs=[pl.BlockSpec(memory_space=pl.ANY),
                   pl.BlockSpec((plan.tb, LANES), lambda i, *_: (i, 0))],
        scratch_shapes=[pltpu.VMEM((2, plan.local_rows, d), BF16),
                        pltpu.VMEM((plan.ffn_tile, d), BF16),
                        pltpu.SemaphoreType.DMA((2,)),
                        pltpu.SemaphoreType.DMA((2,))])
    return pl.pallas_call(
        functools.partial(_dispatch_kernel, plan=plan),
        grid_spec=grid_spec,
        out_shape=[jax.ShapeDtypeStruct((plan.sorted_rows, d), BF16),
                   jax.ShapeDtypeStruct((t, LANES), F32)],
        compiler_params=_cparams(("arbitrary",)),
        name="moe_dispatch",
    )(tabs["seg_src"], tabs["seg_dst"], tabs["seg_rows"], tabs["tail_start"], tabs["tail_rows"], tabs["n_used"],
      h2, meta, loc_col)


def _ffn_kernel(te_ref, nu_ref, x_ref, wg_ref, wu_ref, wd_ref, y_ref, wg_s, wu_s, wd_s):
    j = pl.program_id(0)
    used = j < nu_ref[0]
    new_expert = jnp.logical_or(j == 0, te_ref[j] != te_ref[jnp.maximum(j - 1, 0)])

    @pl.when(jnp.logical_and(used, new_expert))
    def _():
        wg_s[...] = wg_ref[0].astype(BF16)
        wu_s[...] = wu_ref[0].astype(BF16)
        wd_s[...] = wd_ref[0].astype(BF16)

    @pl.when(used)
    def _():
        x = x_ref[...]
        g = jnp.dot(x, wg_s[...], preferred_element_type=F32)
        u = jnp.dot(x, wu_s[...], preferred_element_type=F32)
        y_ref[...] = jnp.dot((_silu(g) * u).astype(BF16), wd_s[...], preferred_element_type=F32).astype(BF16)

    @pl.when(jnp.logical_not(used))
    def _():
        y_ref[...] = jnp.zeros_like(y_ref)


def _ffn(xs, wg, wu, wd, layer, tabs, plan):
    d, de = wg.shape[2], wg.shape[3]
    tm = plan.ffn_tile
    row_blk = lambda j, te, nu: (jnp.minimum(j, nu[0] - 1), 0)
    grid_spec = pltpu.PrefetchScalarGridSpec(
        num_scalar_prefetch=2,
        grid=(plan.ffn_tiles,),
        in_specs=[pl.BlockSpec((tm, d), row_blk),
                  pl.BlockSpec((None, 1, d, de), lambda j, te, nu: (layer, te[j], 0, 0)),
                  pl.BlockSpec((None, 1, d, de), lambda j, te, nu: (layer, te[j], 0, 0)),
                  pl.BlockSpec((None, 1, de, d), lambda j, te, nu: (layer, te[j], 0, 0))],
        out_specs=pl.BlockSpec((tm, d), lambda j, te, nu: (j, 0)),
        scratch_shapes=[pltpu.VMEM((d, de), BF16), pltpu.VMEM((d, de), BF16), pltpu.VMEM((de, d), BF16)])
    return pl.pallas_call(
        _ffn_kernel,
        grid_spec=grid_spec,
        out_shape=jax.ShapeDtypeStruct(xs.shape, BF16),
        compiler_params=_cparams(("arbitrary",)),
        name="moe_ffn",
    )(tabs["tile_expert"], tabs["n_used"], xs, wg, wu, wd)


def _combine_kernel(src_ref, dst_ref, rows_ref, ys_ref, cmeta_ref, x_ref, mod_ref, fw_ref, o_ref, buf_ref, sem,
                    *, plan, final):
    i = pl.program_id(0)
    n_tiles = pl.num_programs(0)
    slot = i % 2

    def fetch(tile, s, start):
        _tile_segments(tile, s, buf_ref, ys_ref, sem, src_ref, dst_ref, rows_ref, False, start)

    @pl.when(i == 0)
    def _():
        buf_ref[...] = jnp.zeros_like(buf_ref)
        fetch(0, 0, True)

    @pl.when(i + 1 < n_tiles)
    def _():
        fetch(i + 1, 1 - slot, True)

    fetch(i, slot, False)

    cm = cmeta_ref[...]
    r1 = cm[:, 0:1] * 256.0 + cm[:, 1:2]
    r2 = cm[:, 2:3] * 256.0 + cm[:, 3:4]
    rid = lax.broadcasted_iota(jnp.int32, (plan.tb, plan.local_rows), 1).astype(F32)
    comb = (jnp.where(rid == r1, cm[:, 4:5], 0.0) + jnp.where(rid == r2, cm[:, 5:6], 0.0)).astype(BF16)
    moe = jnp.dot(comb, buf_ref[slot], preferred_element_type=F32)
    xn = x_ref[...] + mod_ref[5:6, :] * moe
    if final:
        ms = jnp.mean(xn * xn, axis=-1, keepdims=True)
        xn = xn * lax.rsqrt(ms + EPS) * fw_ref[...]
    o_ref[...] = xn


def _combine(ys, cmeta, x1, mod4, layer, seq, tabs, plan, final_w, final):
    t, d = x1.shape
    per_b = seq // plan.tb
    grid_spec = pltpu.PrefetchScalarGridSpec(
        num_scalar_prefetch=3,
        grid=(plan.n_tiles,),
        in_specs=[pl.BlockSpec(memory_space=pl.ANY),
                  pl.BlockSpec((plan.tb, LANES), lambda i, *_: (i, 0)),
                  pl.BlockSpec((plan.tb, d), lambda i, *_: (i, 0)),
                  pl.BlockSpec((None, None, 6, d), lambda i, *_: (layer, i // per_b, 0, 0)),
                  pl.BlockSpec((1, d), lambda i, *_: (0, 0))],
        out_specs=pl.BlockSpec((plan.tb, d), lambda i, *_: (i, 0)),
        scratch_shapes=[pltpu.VMEM((2, plan.local_rows, d), BF16),
                        pltpu.SemaphoreType.DMA((2,))])
    return pl.pallas_call(
        functools.partial(_combine_kernel, plan=plan, final=final),
        grid_spec=grid_spec,
        out_shape=jax.ShapeDtypeStruct((t, d), F32),
        compiler_params=_cparams(("arbitrary",)),
        name="moe_combine",
    )(tabs["seg_src"], tabs["seg_dst"], tabs["seg_rows"], ys, cmeta, x1, mod4, final_w.reshape(1, d))


def kernel(x, c, w_ada, b_ada, norm_mix_w, norm_ffn_w, w_in_even, hg_lb_logits, hg_norm_w, ret_norm_w,
           w_out_even, w_in_odd, w_gla_a2, b_gla_a2, gla_norm_w, w_out_odd, w_router, b_router,
           w_gate, w_up, w_down, final_norm_w):
    bsz, seq, d = x.shape
    depth = w_ada.shape[0]
    x2 = x.reshape(bsz * seq, d)
    mod4 = _adaln_mod(c, w_ada, b_ada).reshape(depth, bsz, 6, d)
    wr_t = w_router.T.astype(BF16)
    br_col = b_router.reshape(N_EXPERTS, 1)
    plan = _MoePlan(bsz * seq, min(MOE_TOKEN_TILE, seq))
    for l in range(depth):
        j = l // 2
        if l % 2 == 0:
            proj, hf = _inproj(x2, mod4, l, seq, norm_mix_w[l], w_in_even[j].astype(BF16),
                               _inproj_even_kernel, w_in_even.shape[-1], [], [])
            o = _even_mixers(proj, hf, hg_lb_logits, hg_norm_w[j], ret_norm_w[j], bsz, seq, l)
            w_out = w_out_even[j].astype(BF16)
        else:
            n_main = w_in_odd.shape[-1] - GLA_RANK
            wi = w_in_odd[j]
            wa = jnp.pad(wi[:, n_main:], ((0, 0), (0, LANES - GLA_RANK))).astype(BF16)
            wa2 = jnp.pad(w_gla_a2[j], ((0, LANES - GLA_RANK), (0, 0))).astype(BF16)
            full = lambda a: pl.BlockSpec(a.shape, lambda i: (0,) * a.ndim)
            ba2 = b_gla_a2[j].reshape(1, -1)
            proj, log_a = _inproj(x2, mod4, l, seq, norm_mix_w[l], wi[:, :n_main].astype(BF16),
                                  _inproj_odd_kernel, n_main, [full(wa), full(wa2), full(ba2)], [wa, wa2, ba2])
            o = _odd_mixer(proj, log_a, gla_norm_w[j], bsz, seq, l)
            w_out = w_out_odd[j].astype(BF16)
        x1, h2, meta, counts = _outproj_route(o, w_out, x2, mod4, l, seq, plan.tb, norm_ffn_w[l], wr_t, br_col)
        tabs = _moe_tables(counts[:, :, 0].astype(jnp.int32), plan)
        xs, cmeta = _dispatch(h2, meta, tabs, plan)
        ys = _ffn(xs, w_gate, w_up, w_down, l, tabs, plan)
        x2 = _combine(ys, cmeta, x1, mod4, l, seq, tabs, plan, final_norm_w, final=(l == depth - 1))
    return x2.reshape(bsz, seq, d)
```

```python
import functools

import jax
import jax.numpy as jnp
from jax import lax
from jax.experimental import pallas as pl
from jax.experimental.pallas import tpu as pltpu

F32 = jnp.float32
BF16 = jnp.bfloat16
HIGHEST = lax.Precision.HIGHEST

EPS = 1e-6
CHUNK = 64
RET_CHUNK = 256
HEAD_DIM = 128
N_HEADS = 4
N_EXPERTS = 16
N_GROUPS = 4
EXPERTS_PER_GROUP = N_EXPERTS // N_GROUPS
GLA_RANK = 16
GLA_GATE_NORM = 16.0
ROPE_BASE = 10000.0
LANES = 128
VMEM_LIMIT = 52 * 1024 * 1024
SUBLANES = 8
INPROJ_ROW_TILE = 512
MIX_SEQ_TILE = 512
TOP_K = 2
ROW_GROUP = 16
MOE_TOKEN_TILE = 512
FFN_ROW_TILE = 1024
PERM_BLOCK = 256


def _cparams(sem):
    return pltpu.CompilerParams(dimension_semantics=sem, vmem_limit_bytes=VMEM_LIMIT)


def _sigmoid(z):
    return 0.5 * jnp.tanh(0.5 * z) + 0.5


def _silu(z):
    return z * _sigmoid(z)


def _mod_kernel(c_ref, w_ref, b_ref, o_ref):
    c = c_ref[...]
    o_ref[0] = jnp.dot(_silu(c), w_ref[0], preferred_element_type=F32, precision=HIGHEST) + b_ref[0]


def _adaln_mod(c, w_ada, b_ada):
    depth, d, n = w_ada.shape
    bsz = c.shape[0]
    tn = 1536
    return pl.pallas_call(
        _mod_kernel,
        grid=(depth, n // tn),
        in_specs=[pl.BlockSpec((bsz, d), lambda l, j: (0, 0)),
                  pl.BlockSpec((1, d, tn), lambda l, j: (l, 0, j)),
                  pl.BlockSpec((1, 1, tn), lambda l, j: (l, 0, j))],
        out_specs=pl.BlockSpec((1, bsz, tn), lambda l, j: (l, 0, j)),
        out_shape=jax.ShapeDtypeStruct((depth, bsz, n), F32),
        compiler_params=_cparams(("arbitrary", "arbitrary")),
        name="adaln_mod",
    )(c, w_ada, b_ada.reshape(depth, 1, n))


def _norm_mod(x, nw, mod_ref, shift_row, scale_row):
    ms = jnp.mean(x * x, axis=-1, keepdims=True)
    return x * lax.rsqrt(ms + EPS) * (nw * (1.0 + mod_ref[scale_row:scale_row + 1, :])) + mod_ref[shift_row:shift_row + 1, :]


def _inproj_even_kernel(x_ref, mod_ref, nw_ref, w_ref, o_ref, f_ref):
    h = _norm_mod(x_ref[...], nw_ref[...], mod_ref, 0, 1)
    res = jnp.dot(h.astype(BF16), w_ref[...], preferred_element_type=F32)
    o_ref[...] = res.astype(BF16)
    f_ref[...] = res[:, 512:1024]


def _log_sigmoid(z):
    return jnp.minimum(z, 0.0) - jnp.log1p(jnp.exp(-jnp.abs(z)))


def _inproj_odd_kernel(x_ref, mod_ref, nw_ref, w_ref, wa_ref, wa2_ref, ba2_ref, o_ref, la_ref):
    h = _norm_mod(x_ref[...], nw_ref[...], mod_ref, 0, 1).astype(BF16)
    o_ref[...] = jnp.dot(h, w_ref[...], preferred_element_type=F32).astype(BF16)
    ga = jnp.dot(h, wa_ref[...], preferred_element_type=F32)
    z = jnp.dot(ga.astype(BF16), wa2_ref[...], preferred_element_type=F32) + ba2_ref[...]
    la_ref[...] = _log_sigmoid(z) * (1.0 / GLA_GATE_NORM)


def _inproj(x2, mod4, layer, seq, nw, weights, kernel_fn, n_main, extra_specs, extra_args):
    t, d = x2.shape
    tm = min(INPROJ_ROW_TILE, seq)
    per_b = seq // tm
    return pl.pallas_call(
        kernel_fn,
        grid=(t // tm,),
        in_specs=[pl.BlockSpec((tm, d), lambda i: (i, 0)),
                  pl.BlockSpec((None, None, 6, d), lambda i: (layer, i // per_b, 0, 0)),
                  pl.BlockSpec((1, d), lambda i: (0, 0)),
                  pl.BlockSpec((d, n_main), lambda i: (0, 0))] + extra_specs,
        out_specs=[pl.BlockSpec((tm, n_main), lambda i: (i, 0)),
                   pl.BlockSpec((tm, 512), lambda i: (i, 0))],
        out_shape=[jax.ShapeDtypeStruct((t, n_main), BF16),
                   jax.ShapeDtypeStruct((t, 512), F32)],
        compiler_params=_cparams(("parallel",)),
        name=f"inproj_l{layer}",
    )(x2, mod4, nw.reshape(1, d), weights, *extra_args)


def _causal_mask():
    r = lax.broadcasted_iota(jnp.int32, (CHUNK, CHUNK), 0)
    c = lax.broadcasted_iota(jnp.int32, (CHUNK, CHUNK), 1)
    return r >= c


def _dot_nt(a, b):
    return lax.dot_general(a, b, (((1,), (1,)), ((), ())), preferred_element_type=F32)


def _dot_tn(a, b):
    return lax.dot_general(a, b, (((0,), (0,)), ((), ())), preferred_element_type=F32)


def _head_out(o, g, nw, n_norm):
    ms = jnp.sum(o * o, axis=-1, keepdims=True) * (1.0 / n_norm)
    return o * lax.rsqrt(ms + EPS) * nw * _silu(g)


def _cumsum_rows(lg):
    tril = jnp.where(_causal_mask(), 1.0, 0.0).astype(BF16)
    hi = lg.astype(BF16)
    lo = (lg - hi.astype(F32)).astype(BF16)
    return jnp.dot(tril, hi, preferred_element_type=F32) + jnp.dot(tril, lo, preferred_element_type=F32)


def _decay_heads(q, k, v, b, st_ref, st_base, dv):
    b_mid = b[CHUNK // 2 - 1:CHUNK // 2, :]
    b_last = b[CHUNK - 1:CHUNK, :]
    qe = q * jnp.exp(b - b_mid)
    ke = k * jnp.exp(b_mid - b)
    qs = qe.astype(BF16)
    ks = ke.astype(BF16)
    qi = (qe * jnp.exp(b_mid)).astype(BF16)
    kd = (ke * jnp.exp(b_last - b_mid)).astype(BF16)
    dec = jnp.exp(b_last)
    mask = _causal_mask()
    outs = []
    for h in range(N_HEADS):
        kl = slice(h * HEAD_DIM, (h + 1) * HEAD_DIM)
        vh = v[:, h * dv:(h + 1) * dv]
        scores = jnp.where(mask, _dot_nt(qs[:, kl], ks[:, kl]), 0.0)
        o = jnp.dot(scores.astype(BF16), vh, preferred_element_type=F32)
        st = st_ref[st_base + h]
        o = o + _dot_nt(qi[:, kl], st.astype(BF16))
        st_ref[st_base + h] = dec[:, kl] * st + _dot_tn(vh, kd[:, kl])
        outs.append(o)
    return outs


def _retention_log_decay():
    headf = (lax.broadcasted_iota(jnp.int32, (1, N_HEADS * HEAD_DIM), 1) // HEAD_DIM).astype(F32)
    return jnp.log1p(-jnp.exp2(-5.0 - headf))


def _even_mixer_kernel(proj_ref, hf_ref, lbl_ref, hgnw_ref, retnw_ref, cos_ref, sin_ref, o_ref,
                       st_ref, eq_ref, ek_ref, dm_ref, *, layer, n_chunks, ret_chunk):
    hd, w = HEAD_DIM, N_HEADS * HEAD_DIM
    lg = _retention_log_decay()

    @pl.when(jnp.logical_and(pl.program_id(0) == 0, pl.program_id(1) == 0))
    def _():
        pos = lax.broadcasted_iota(jnp.int32, (ret_chunk, w), 0).astype(F32)
        eq_ref[...] = jnp.exp((pos + 1.0) * lg)
        ek_ref[...] = jnp.exp((ret_chunk - 1.0 - pos) * lg)
        r = lax.broadcasted_iota(jnp.int32, (ret_chunk, ret_chunk), 0)
        cc = lax.broadcasted_iota(jnp.int32, (ret_chunk, ret_chunk), 1)
        dt = (r - cc).astype(F32)
        for h in range(N_HEADS):
            dm_ref[h] = jnp.where(r >= cc, jnp.exp(dt * lg[:, h * hd:h * hd + 1]), 0.0)

    @pl.when(pl.program_id(1) == 0)
    def _():
        st_ref[...] = jnp.zeros_like(st_ref)

    lbl = lbl_ref[...]
    e = jnp.exp(lbl - jnp.max(lbl, axis=0, keepdims=True))
    p = e / jnp.sum(e, axis=0, keepdims=True)
    lb = jnp.sum(p[:layer + 1], axis=0, keepdims=True)
    hgnw = hgnw_ref[...]
    retnw = retnw_ref[...]
    e_c = jnp.exp(ret_chunk * lg)

    def rot(xx, cs, sn):
        return xx * cs + pltpu.roll(xx, hd // 2, 1) * sn

    for c in range(n_chunks):
        sl = pl.ds(c * CHUNK, CHUNK)
        f = lb + (1.0 - lb) * _sigmoid(hf_ref[sl, :])
        b = _cumsum_rows(jnp.log(f))
        q = _sigmoid(proj_ref[sl, 0:w].astype(F32))
        outs = _decay_heads(q, 1.0 - f, proj_ref[sl, 2 * w:3 * w], b, st_ref, 0, hd)
        for h, o in enumerate(outs):
            cl = slice(h * hd, (h + 1) * hd)
            g = proj_ref[sl, 3 * w + h * hd:3 * w + (h + 1) * hd].astype(F32)
            o_ref[sl, cl] = _head_out(o, g, hgnw[:, cl], hd).astype(BF16)

    for c in range(n_chunks * CHUNK // ret_chunk):
        sl = pl.ds(c * ret_chunk, ret_chunk)
        cs = cos_ref[sl, :]
        sn = sin_ref[sl, :]
        for h in range(N_HEADS):
            cl = slice(h * hd, (h + 1) * hd)
            qh = rot(proj_ref[sl, 4 * w + h * hd:4 * w + (h + 1) * hd].astype(F32), cs, sn)
            kh = rot(proj_ref[sl, 5 * w + h * hd:5 * w + (h + 1) * hd].astype(F32), cs, sn) * (hd ** -0.5)
            vh = proj_ref[sl, 6 * w + h * hd:6 * w + (h + 1) * hd]
            scores = _dot_nt(qh.astype(BF16), kh.astype(BF16)) * dm_ref[h]
            o = jnp.dot(scores.astype(BF16), vh, preferred_element_type=F32)
            st = st_ref[N_HEADS + h]
            o = o + _dot_nt((qh * eq_ref[:, cl]).astype(BF16), st.astype(BF16))
            st_ref[N_HEADS + h] = e_c[:, cl] * st + _dot_tn(vh, (kh * ek_ref[:, cl]).astype(BF16))
            g = proj_ref[sl, 7 * w + h * hd:7 * w + (h + 1) * hd].astype(F32)
            o_ref[sl, w + h * hd:w + (h + 1) * hd] = _head_out(o, g, retnw[:, cl], hd).astype(BF16)


def _gla_mixer_kernel(proj_ref, la_ref, nw_ref, o_ref, st_ref, *, n_chunks):
    hd, w = HEAD_DIM, N_HEADS * HEAD_DIM
    dv = 2 * hd

    @pl.when(pl.program_id(1) == 0)
    def _():
        st_ref[...] = jnp.zeros_like(st_ref)

    nw = nw_ref[...]
    for c in range(n_chunks):
        sl = pl.ds(c * CHUNK, CHUNK)
        b = _cumsum_rows(la_ref[sl, :])
        q = proj_ref[sl, 0:w].astype(F32) * (hd ** -0.5)
        k = proj_ref[sl, w:2 * w].astype(F32)
        outs = _decay_heads(q, k, proj_ref[sl, 2 * w:2 * w + N_HEADS * dv], b, st_ref, 0, dv)
        for h, o in enumerate(outs):
            cl = slice(h * dv, (h + 1) * dv)
            g = proj_ref[sl, 4 * w + h * dv:4 * w + (h + 1) * dv].astype(F32)
            o_ref[sl, cl] = _head_out(o, g, nw[:, cl], dv).astype(BF16)


def _mixer_call(kernel_fn, bsz, seq, ts, in_specs, scratch_shapes, name, args):
    d_out = N_HEADS * 2 * HEAD_DIM
    return pl.pallas_call(
        kernel_fn,
        grid=(bsz, seq // ts),
        in_specs=in_specs,
        out_specs=pl.BlockSpec((ts, d_out), lambda b, s: (b * (seq // ts) + s, 0)),
        out_shape=jax.ShapeDtypeStruct((bsz * seq, d_out), BF16),
        scratch_shapes=scratch_shapes,
        compiler_params=_cparams(("arbitrary", "arbitrary")),
        name=name,
    )(*args)


def _even_mixers(proj, hf, lb_logits, hg_nw, ret_nw, bsz, seq, layer):
    ts = min(MIX_SEQ_TILE, seq)
    rc = min(RET_CHUNK, ts)
    per_b = seq // ts
    hd, w = HEAD_DIM, N_HEADS * HEAD_DIM
    rows = lambda width: pl.BlockSpec((ts, width), lambda b, s: (b * per_b + s, 0))
    full = lambda a: pl.BlockSpec(a.shape, lambda b, s: (0,) * a.ndim)
    inv = ROPE_BASE ** (-jnp.arange(0, hd, 2, dtype=F32) / hd)
    ang = jnp.arange(seq, dtype=F32)[:, None] * inv[None, :]
    cos2 = jnp.concatenate([jnp.cos(ang), jnp.cos(ang)], axis=-1)
    sin2 = jnp.concatenate([-jnp.sin(ang), jnp.sin(ang)], axis=-1)
    tab = pl.BlockSpec((ts, hd), lambda b, s: (s, 0))
    hg_nw, ret_nw = hg_nw.reshape(1, -1), ret_nw.reshape(1, -1)
    scratch = [pltpu.VMEM((2 * N_HEADS, hd, hd), F32), pltpu.VMEM((rc, w), F32), pltpu.VMEM((rc, w), F32),
               pltpu.VMEM((N_HEADS, rc, rc), F32)]
    return _mixer_call(
        functools.partial(_even_mixer_kernel, layer=layer, n_chunks=ts // CHUNK, ret_chunk=rc), bsz, seq, ts,
        [rows(proj.shape[1]), rows(hf.shape[1]), full(lb_logits), full(hg_nw), full(ret_nw), tab, tab],
        scratch, f"even_mixer_l{layer}",
        (proj, hf, lb_logits, hg_nw, ret_nw, cos2, sin2))


def _odd_mixer(proj, log_a, gla_nw, bsz, seq, layer):
    ts = min(MIX_SEQ_TILE, seq)
    per_b = seq // ts
    rows = lambda width: pl.BlockSpec((ts, width), lambda b, s: (b * per_b + s, 0))
    gla_nw = gla_nw.reshape(1, -1)
    return _mixer_call(
        functools.partial(_gla_mixer_kernel, n_chunks=ts // CHUNK), bsz, seq, ts,
        [rows(proj.shape[1]), rows(log_a.shape[1]), pl.BlockSpec(gla_nw.shape, lambda b, s: (0, 0))],
        [pltpu.VMEM((N_HEADS, 2 * HEAD_DIM, HEAD_DIM), F32)], f"gla_mixer_l{layer}",
        (proj, log_a, gla_nw))


def _row_pick(n_rows, rows):
    rid = lax.broadcasted_iota(jnp.int32, (n_rows, rows[0].shape[1]), 0)
    out = jnp.zeros(rid.shape, F32)
    for j, row in reversed(list(enumerate(rows))):
        out = jnp.where(rid == j, row, out)
    return out


def _route(h2, wr_t, br):
    neg = -jnp.inf
    logits = _dot_nt(wr_t, h2)
    tm = logits.shape[1]
    eid = lax.broadcasted_iota(jnp.int32, logits.shape, 0)
    ex = jnp.exp(logits - jnp.max(logits, axis=0, keepdims=True))
    probs = ex / jnp.sum(ex, axis=0, keepdims=True)
    sel = probs + br
    grp = eid // EXPERTS_PER_GROUP
    gs = [jnp.max(jnp.where(grp == g, sel, neg), axis=0, keepdims=True) for g in range(N_GROUPS)]
    best = functools.reduce(jnp.maximum, gs)
    gidx = jnp.full(best.shape, N_GROUPS - 1, jnp.int32)
    for g in range(N_GROUPS - 2, -1, -1):
        gidx = jnp.where(gs[g] == best, g, gidx)
    masked = jnp.where(grp == gidx, sel, neg)
    m1 = jnp.max(masked, axis=0, keepdims=True)
    i1 = jnp.min(jnp.where(masked == m1, eid, N_EXPERTS), axis=0, keepdims=True)
    masked2 = jnp.where(eid == i1, neg, masked)
    m2 = jnp.max(masked2, axis=0, keepdims=True)
    i2 = jnp.min(jnp.where(masked2 == m2, eid, N_EXPERTS), axis=0, keepdims=True)
    p1 = jnp.sum(jnp.where(eid == i1, probs, 0.0), axis=0, keepdims=True)
    p2 = jnp.sum(jnp.where(eid == i2, probs, 0.0), axis=0, keepdims=True)
    den = p1 + p2
    chosen = jnp.where(eid == i1, 1.0, jnp.where(eid == i2, 1.0, 0.0))
    r = lax.broadcasted_iota(jnp.int32, (tm, tm), 0)
    c = lax.broadcasted_iota(jnp.int32, (tm, tm), 1)
    earlier = jnp.where(r < c, 1.0, 0.0).astype(BF16)
    rank = jnp.dot(chosen.astype(BF16), earlier, preferred_element_type=F32)
    rank1 = jnp.sum(jnp.where(eid == i1, rank, 0.0), axis=0, keepdims=True)
    rank2 = jnp.sum(jnp.where(eid == i2, rank, 0.0), axis=0, keepdims=True)
    meta = _row_pick(SUBLANES, [i1.astype(F32), i2.astype(F32), p1 / den, p2 / den, rank1, rank2])
    return meta, jnp.sum(chosen, axis=1, keepdims=True)


def _outproj_kernel(o_ref, w_ref, x_ref, mod_ref, nw_ref, wr_ref, br_ref, x1_ref, h2_ref, meta_ref, cnt_ref):
    y = jnp.dot(o_ref[...], w_ref[...], preferred_element_type=F32)
    x1 = x_ref[...] + mod_ref[2:3, :] * y
    x1_ref[...] = x1
    h2 = _norm_mod(x1, nw_ref[...], mod_ref, 3, 4).astype(BF16)
    h2_ref[...] = h2
    meta, counts = _route(h2, wr_ref[...], br_ref[...])
    meta_ref[...] = meta
    cnt_ref[...] = jnp.broadcast_to(counts, cnt_ref.shape)


def _outproj_route(o, w, x2, mod4, layer, seq, tb, ffn_nw, wr_t, br_col):
    t, d = x2.shape
    per_b = seq // tb
    full = lambda a: pl.BlockSpec(a.shape, lambda i: (0,) * a.ndim)
    rows = lambda width: pl.BlockSpec((tb, width), lambda i: (i, 0))
    return pl.pallas_call(
        _outproj_kernel,
        grid=(t // tb,),
        in_specs=[rows(o.shape[1]), full(w), rows(d),
                  pl.BlockSpec((None, None, 6, d), lambda i: (layer, i // per_b, 0, 0)),
                  pl.BlockSpec((1, d), lambda i: (0, 0)), full(wr_t), full(br_col)],
        out_specs=[rows(d), rows(d),
                   pl.BlockSpec((None, SUBLANES, tb), lambda i: (i, 0, 0)),
                   pl.BlockSpec((None, N_EXPERTS, LANES), lambda i: (i, 0, 0))],
        out_shape=[jax.ShapeDtypeStruct((t, d), F32),
                   jax.ShapeDtypeStruct((t, d), BF16),
                   jax.ShapeDtypeStruct((t // tb, SUBLANES, tb), F32),
                   jax.ShapeDtypeStruct((t // tb, N_EXPERTS, LANES), F32)],
        compiler_params=_cparams(("parallel",)),
        name=f"outproj_route_l{layer}",
    )(o, w, x2, mod4, ffn_nw.reshape(1, d), wr_t, br_col)


class _MoePlan:
    def __init__(self, t, tb):
        self.tb = tb
        self.n_tiles = t // tb
        self.local_rows = -(-(TOP_K * tb + N_EXPERTS * (ROW_GROUP - 1)) // PERM_BLOCK) * PERM_BLOCK
        self.ffn_tile = min(FFN_ROW_TILE, TOP_K * tb)
        worst_rows = TOP_K * t + self.n_tiles * N_EXPERTS * (ROW_GROUP - 1)
        self.ffn_tiles = -(-worst_rows // self.ffn_tile) + N_EXPERTS
        self.sorted_rows = self.ffn_tiles * self.ffn_tile


def _moe_tables(counts, plan):
    i32 = jnp.int32
    n_pad = (counts + (ROW_GROUP - 1)) // ROW_GROUP * ROW_GROUP
    ends = jnp.cumsum(n_pad, axis=1)
    loc = ends - n_pad
    n_e = jnp.sum(n_pad, axis=0)
    n_e_pad = -(-n_e // plan.ffn_tile) * plan.ffn_tile
    region_end = jnp.cumsum(n_e_pad)
    base = region_end - n_e_pad
    goff = base[None, :] + jnp.cumsum(n_pad, axis=0) - n_pad
    n_used = region_end[-1] // plan.ffn_tile
    tile_row = jnp.minimum(jnp.arange(plan.ffn_tiles, dtype=i32), n_used - 1) * plan.ffn_tile
    tile_expert = jnp.minimum(jnp.sum(region_end[None, :] <= tile_row[:, None], axis=-1), N_EXPERTS - 1)
    flat = lambda a: a.reshape(-1).astype(i32)
    return dict(loc=loc, seg_src=flat(loc), seg_dst=flat(goff), seg_rows=flat(n_pad),
                tail_start=(base + n_e).astype(i32), tail_rows=(n_e_pad - n_e).astype(i32),
                tile_expert=tile_expert.astype(i32), n_used=n_used.reshape(1).astype(i32))


def _segment_copy(buf_ref, slot, src_row, hbm_ref, dst_row, n_rows, sem, to_hbm):
    al = lambda v: pl.multiple_of(v, ROW_GROUP)
    local = buf_ref.at[slot, pl.ds(al(src_row), al(n_rows))]
    remote = hbm_ref.at[pl.ds(al(dst_row), al(n_rows))]
    src, dst = (local, remote) if to_hbm else (remote, local)
    return pltpu.make_async_copy(src, dst, sem.at[slot])


def _tile_segments(tile, slot, buf_ref, hbm_ref, sem, src_ref, dst_ref, rows_ref, to_hbm, start):
    def seg(e, c):
        k = tile * N_EXPERTS + e
        n = rows_ref[k]

        @pl.when(n > 0)
        def _():
            cp = _segment_copy(buf_ref, slot, src_ref[k], hbm_ref, dst_ref[k], n, sem, to_hbm)
            cp.start() if start else cp.wait()
        return c

    lax.fori_loop(0, N_EXPERTS, seg, 0)


def _dispatch_kernel(src_ref, dst_ref, rows_ref, tstart_ref, trows_ref, nu_ref, h_ref, meta_ref, loc_ref,
                     sorted_ref, cmeta_ref, buf_ref, zero_ref, sem, zsem, *, plan):
    i = pl.program_id(0)
    last = pl.num_programs(0) - 1
    slot = i % 2
    tb, rows_l = plan.tb, plan.local_rows

    def tails(start):
        def tail(e, c):
            n = trows_ref[e]

            @pl.when(n > 0)
            def _():
                cp = pltpu.make_async_copy(zero_ref.at[pl.ds(0, pl.multiple_of(n, ROW_GROUP))],
                                           sorted_ref.at[pl.ds(pl.multiple_of(tstart_ref[e], ROW_GROUP),
                                                               pl.multiple_of(n, ROW_GROUP))], zsem.at[0])
                cp.start() if start else cp.wait()
            return c

        lax.fori_loop(0, N_EXPERTS, tail, 0)

    def spare_copy(j):
        row = j * plan.ffn_tile if isinstance(j, int) else pl.multiple_of(j * plan.ffn_tile, plan.ffn_tile)
        return pltpu.make_async_copy(zero_ref, sorted_ref.at[pl.ds(row, plan.ffn_tile)], zsem.at[1])

    @pl.when(i == 0)
    def _():
        zero_ref[...] = jnp.zeros_like(zero_ref)
        tails(True)
        lax.fori_loop(nu_ref[0], plan.ffn_tiles, lambda j, c: (spare_copy(j).start(), c)[1], 0)

    def wait_tile(tile, s):
        _tile_segments(tile, s, buf_ref, sorted_ref, sem, src_ref, dst_ref, rows_ref, True, False)

    @pl.when(i >= 2)
    def _():
        wait_tile(i - 2, slot)

    meta = meta_ref[...]
    eid = lax.broadcasted_iota(jnp.int32, (N_EXPERTS, tb), 0).astype(F32)
    loc_col = loc_ref[...]

    def local_row(e_row, rank_row):
        return jnp.sum(jnp.where(eid == e_row, loc_col, 0.0), axis=0, keepdims=True) + rank_row

    r1_row = local_row(meta[0:1, :], meta[4:5, :])
    r2_row = local_row(meta[1:2, :], meta[5:6, :])
    r1h, r2h = jnp.floor(r1_row * (1.0 / 256.0)), jnp.floor(r2_row * (1.0 / 256.0))
    pieces = _row_pick(LANES, [r1h, r1_row - 256.0 * r1h, r2h, r2_row - 256.0 * r2h,
                               meta[2:3, :], meta[3:4, :]]).astype(BF16)
    eye = jnp.where(lax.broadcasted_iota(jnp.int32, (tb, tb), 0) == lax.broadcasted_iota(jnp.int32, (tb, tb), 1),
                    1.0, 0.0).astype(BF16)
    cmeta_ref[...] = _dot_nt(eye, pieces)
    h = h_ref[...]
    blk = PERM_BLOCK
    for rb in range(rows_l // blk):
        rid = (lax.broadcasted_iota(jnp.int32, (blk, tb), 0) + rb * blk).astype(F32)
        perm = jnp.where(rid == r1_row, 1.0, jnp.where(rid == r2_row, 1.0, 0.0)).astype(BF16)
        buf_ref[slot, rb * blk:(rb + 1) * blk, :] = jnp.dot(perm, h, preferred_element_type=F32).astype(BF16)

    _tile_segments(i, slot, buf_ref, sorted_ref, sem, src_ref, dst_ref, rows_ref, True, True)

    @pl.when(i == last)
    def _():
        @pl.when(i >= 1)
        def _():
            wait_tile(i - 1, 1 - slot)
        wait_tile(i, slot)
        tails(False)
        lax.fori_loop(nu_ref[0], plan.ffn_tiles, lambda j, c: (spare_copy(0).wait(), c)[1], 0)


def _dispatch(h2, meta, tabs, plan):
    t, d = h2.shape
    loc_col = tabs["loc"].astype(F32)[:, :, None]
    grid_spec = pltpu.PrefetchScalarGridSpec(
        num_scalar_prefetch=6,
        grid=(plan.n_tiles,),
        in_specs=[pl.BlockSpec((plan.tb, d), lambda i, *_: (i, 0)),
                  pl.BlockSpec((None, SUBLANES, plan.tb), lambda i, *_: (i, 0, 0)),
                  pl.BlockSpec((None, N_EXPERTS, 1), lambda i, *_: (i, 0, 0))],
        out_specs=[pl.BlockSpec(memory_space=pl.ANY),
                   pl.BlockSpec((plan.tb, LANES), lambda i, *_: (i, 0))],
        scratch_shapes=[pltpu.VMEM((2, plan.local_rows, d), BF16),
                        pltpu.VMEM((plan.ffn_tile, d), BF16),
                        pltpu.SemaphoreType.DMA((2,)),
                        pltpu.SemaphoreType.DMA((2,))])
    return pl.pallas_call(
        functools.partial(_dispatch_kernel, plan=plan),
        grid_spec=grid_spec,
        out_shape=[jax.ShapeDtypeStruct((plan.sorted_rows, d), BF16),
                   jax.ShapeDtypeStruct((t, LANES), F32)],
        compiler_params=_cparams(("arbitrary",)),
        name="moe_dispatch",
    )(tabs["seg_src"], tabs["seg_dst"], tabs["seg_rows"], tabs["tail_start"], tabs["tail_rows"], tabs["n_used"],
      h2, meta, loc_col)


def _ffn_kernel(te_ref, nu_ref, x_ref, wg_ref, wu_ref, wd_ref, y_ref, wg_s, wu_s, wd_s):
    j = pl.program_id(0)
    used = j < nu_ref[0]
    new_expert = jnp.logical_or(j == 0, te_ref[j] != te_ref[jnp.maximum(j - 1, 0)])

    @pl.when(jnp.logical_and(used, new_expert))
    def _():
        wg_s[...] = wg_ref[0].astype(BF16)
        wu_s[...] = wu_ref[0].astype(BF16)
        wd_s[...] = wd_ref[0].astype(BF16)

    @pl.when(used)
    def _():
        x = x_ref[...]
        g = jnp.dot(x, wg_s[...], preferred_element_type=F32)
        u = jnp.dot(x, wu_s[...], preferred_element_type=F32)
        y_ref[...] = jnp.dot((_silu(g) * u).astype(BF16), wd_s[...], preferred_element_type=F32).astype(BF16)

    @pl.when(jnp.logical_not(used))
    def _():
        y_ref[...] = jnp.zeros_like(y_ref)


def _ffn(xs, wg, wu, wd, layer, tabs, plan):
    d, de = wg.shape[2], wg.shape[3]
    tm = plan.ffn_tile
    row_blk = lambda j, te, nu: (jnp.minimum(j, nu[0] - 1), 0)
    grid_spec = pltpu.PrefetchScalarGridSpec(
        num_scalar_prefetch=2,
        grid=(plan.ffn_tiles,),
        in_specs=[pl.BlockSpec((tm, d), row_blk),
                  pl.BlockSpec((None, 1, d, de), lambda j, te, nu: (layer, te[j], 0, 0)),
                  pl.BlockSpec((None, 1, d, de), lambda j, te, nu: (layer, te[j], 0, 0)),
                  pl.BlockSpec((None, 1, de, d), lambda j, te, nu: (layer, te[j], 0, 0))],
        out_specs=pl.BlockSpec((tm, d), lambda j, te, nu: (j, 0)),
        scratch_shapes=[pltpu.VMEM((d, de), BF16), pltpu.VMEM((d, de), BF16), pltpu.VMEM((de, d), BF16)])
    return pl.pallas_call(
        _ffn_kernel,
        grid_spec=grid_spec,
        out_shape=jax.ShapeDtypeStruct(xs.shape, BF16),
        compiler_params=_cparams(("arbitrary",)),
        name="moe_ffn",
    )(tabs["tile_expert"], tabs["n_used"], xs, wg, wu, wd)


def _combine_kernel(src_ref, dst_ref, rows_ref, ys_ref, cm0_ref, cmn_ref, x_ref, mod_ref, fw_ref, o_ref,
                    buf_ref, comb_a, comb_b, sem, *, plan, final):
    comb_refs = (comb_a, comb_b)
    i = pl.program_id(0)
    n_tiles = pl.num_programs(0)
    slot = i % 2

    def fetch(tile, s, start):
        _tile_segments(tile, s, buf_ref, ys_ref, sem, src_ref, dst_ref, rows_ref, False, start)

    @pl.when(i == 0)
    def _():
        buf_ref[...] = jnp.zeros_like(buf_ref)
        fetch(0, 0, True)

    @pl.when(i + 1 < n_tiles)
    def _():
        fetch(i + 1, 1 - slot, True)

    fetch(i, slot, False)

    def build(cm_ref, comb_ref):
        cm = cm_ref[...]
        r1 = cm[:, 0:1] * 256.0 + cm[:, 1:2]
        r2 = cm[:, 2:3] * 256.0 + cm[:, 3:4]
        rid = lax.broadcasted_iota(jnp.int32, (plan.tb, plan.local_rows), 1).astype(F32)
        comb_ref[...] = (jnp.where(rid == r1, cm[:, 4:5], 0.0)
                         + jnp.where(rid == r2, cm[:, 5:6], 0.0)).astype(BF16)

    @pl.when(i == 0)
    def _():
        build(cm0_ref, comb_refs[0])

    def step(parity):
        build(cmn_ref, comb_refs[1 - parity])
        moe = jnp.dot(comb_refs[parity][...], buf_ref[parity], preferred_element_type=F32)
        xn = x_ref[...] + mod_ref[5:6, :] * moe
        if final:
            ms = jnp.mean(xn * xn, axis=-1, keepdims=True)
            xn = xn * lax.rsqrt(ms + EPS) * fw_ref[...]
        o_ref[...] = xn

    for parity in range(2):
        pl.when(slot == parity)(functools.partial(step, parity))


def _combine(ys, cmeta, x1, mod4, layer, seq, tabs, plan, final_w, final):
    t, d = x1.shape
    per_b = seq // plan.tb
    grid_spec = pltpu.PrefetchScalarGridSpec(
        num_scalar_prefetch=3,
        grid=(plan.n_tiles,),
        in_specs=[pl.BlockSpec(memory_space=pl.ANY),
                  pl.BlockSpec((plan.tb, LANES), lambda i, *_: (0, 0)),
                  pl.BlockSpec((plan.tb, LANES), lambda i, *_: (jnp.minimum(i + 1, plan.n_tiles - 1), 0)),
                  pl.BlockSpec((plan.tb, d), lambda i, *_: (i, 0)),
                  pl.BlockSpec((None, None, 6, d), lambda i, *_: (layer, i // per_b, 0, 0)),
                  pl.BlockSpec((1, d), lambda i, *_: (0, 0))],
        out_specs=pl.BlockSpec((plan.tb, d), lambda i, *_: (i, 0)),
        scratch_shapes=[pltpu.VMEM((2, plan.local_rows, d), BF16),
                        pltpu.VMEM((plan.tb, plan.local_rows), BF16),
                        pltpu.VMEM((plan.tb, plan.local_rows), BF16),
                        pltpu.SemaphoreType.DMA((2,))])
    return pl.pallas_call(
        functools.partial(_combine_kernel, plan=plan, final=final),
        grid_spec=grid_spec,
        out_shape=jax.ShapeDtypeStruct((t, d), F32),
        compiler_params=_cparams(("arbitrary",)),
        name="moe_combine",
    )(tabs["seg_src"], tabs["seg_dst"], tabs["seg_rows"], ys, cmeta, cmeta, x1, mod4, final_w.reshape(1, d))


def kernel(x, c, w_ada, b_ada, norm_mix_w, norm_ffn_w, w_in_even, hg_lb_logits, hg_norm_w, ret_norm_w,
           w_out_even, w_in_odd, w_gla_a2, b_gla_a2, gla_norm_w, w_out_odd, w_router, b_router,
           w_gate, w_up, w_down, final_norm_w):
    bsz, seq, d = x.shape
    depth = w_ada.shape[0]
    x2 = x.reshape(bsz * seq, d)
    mod4 = _adaln_mod(c, w_ada, b_ada).reshape(depth, bsz, 6, d)
    wr_t = w_router.T.astype(BF16)
    br_col = b_router.reshape(N_EXPERTS, 1)
    plan = _MoePlan(bsz * seq, min(MOE_TOKEN_TILE, seq))
    for l in range(depth):
        j = l // 2
        if l % 2 == 0:
            proj, hf = _inproj(x2, mod4, l, seq, norm_mix_w[l], w_in_even[j].astype(BF16),
                               _inproj_even_kernel, w_in_even.shape[-1], [], [])
            o = _even_mixers(proj, hf, hg_lb_logits, hg_norm_w[j], ret_norm_w[j], bsz, seq, l)
            w_out = w_out_even[j].astype(BF16)
        else:
            n_main = w_in_odd.shape[-1] - GLA_RANK
            wi = w_in_odd[j]
            wa = jnp.pad(wi[:, n_main:], ((0, 0), (0, LANES - GLA_RANK))).astype(BF16)
            wa2 = jnp.pad(w_gla_a2[j], ((0, LANES - GLA_RANK), (0, 0))).astype(BF16)
            full = lambda a: pl.BlockSpec(a.shape, lambda i: (0,) * a.ndim)
            ba2 = b_gla_a2[j].reshape(1, -1)
            proj, log_a = _inproj(x2, mod4, l, seq, norm_mix_w[l], wi[:, :n_main].astype(BF16),
                                  _inproj_odd_kernel, n_main, [full(wa), full(wa2), full(ba2)], [wa, wa2, ba2])
            o = _odd_mixer(proj, log_a, gla_norm_w[j], bsz, seq, l)
            w_out = w_out_odd[j].astype(BF16)
        x1, h2, meta, counts = _outproj_route(o, w_out, x2, mod4, l, seq, plan.tb, norm_ffn_w[l], wr_t, br_col)
        tabs = _moe_tables(counts[:, :, 0].astype(jnp.int32), plan)
        xs, cmeta = _dispatch(h2, meta, tabs, plan)
        ys = _ffn(xs, w_gate, w_up, w_down, l, tabs, plan)
        x2 = _combine(ys, cmeta, x1, mod4, l, seq, tabs, plan, final_norm_w, final=(l == depth - 1))
    return x2.reshape(bsz, seq, d)
```

```python
import functools

import jax
import jax.numpy as jnp
from jax import lax
from jax.experimental import pallas as pl
from jax.experimental.pallas import tpu as pltpu

F32 = jnp.float32
BF16 = jnp.bfloat16
HIGHEST = lax.Precision.HIGHEST

EPS = 1e-6
CHUNK = 64
RET_CHUNK = 256
HEAD_DIM = 128
N_HEADS = 4
N_EXPERTS = 16
N_GROUPS = 4
EXPERTS_PER_GROUP = N_EXPERTS // N_GROUPS
GLA_RANK = 16
GLA_GATE_NORM = 16.0
ROPE_BASE = 10000.0
LANES = 128
VMEM_LIMIT = 52 * 1024 * 1024
SUBLANES = 8
INPROJ_ROW_TILE = 512
MIX_SEQ_TILE = 512
TOP_K = 2
ROW_GROUP = 16
MOE_TOKEN_TILE = 512
FFN_ROW_TILE = 1024
PERM_BLOCK = 256


def _cparams(sem):
    return pltpu.CompilerParams(dimension_semantics=sem, vmem_limit_bytes=VMEM_LIMIT)


def _sigmoid(z):
    return 0.5 * jnp.tanh(0.5 * z) + 0.5


def _silu(z):
    return z * _sigmoid(z)


def _mod_kernel(c_ref, w_ref, b_ref, o_ref):
    c = c_ref[...]
    o_ref[0] = jnp.dot(_silu(c), w_ref[0], preferred_element_type=F32, precision=HIGHEST) + b_ref[0]


def _adaln_mod(c, w_ada, b_ada):
    depth, d, n = w_ada.shape
    bsz = c.shape[0]
    tn = 1536
    return pl.pallas_call(
        _mod_kernel,
        grid=(depth, n // tn),
        in_specs=[pl.BlockSpec((bsz, d), lambda l, j: (0, 0)),
                  pl.BlockSpec((1, d, tn), lambda l, j: (l, 0, j)),
                  pl.BlockSpec((1, 1, tn), lambda l, j: (l, 0, j))],
        out_specs=pl.BlockSpec((1, bsz, tn), lambda l, j: (l, 0, j)),
        out_shape=jax.ShapeDtypeStruct((depth, bsz, n), F32),
        compiler_params=_cparams(("arbitrary", "arbitrary")),
        name="adaln_mod",
    )(c, w_ada, b_ada.reshape(depth, 1, n))


def _norm_mod(x, nw, mod_ref, shift_row, scale_row):
    ms = jnp.mean(x * x, axis=-1, keepdims=True)
    return x * lax.rsqrt(ms + EPS) * (nw * (1.0 + mod_ref[scale_row:scale_row + 1, :])) + mod_ref[shift_row:shift_row + 1, :]


def _inproj_even_body(h, w_refs, out_refs):
    (w_ref,), (o_ref, f_ref) = w_refs, out_refs
    res = jnp.dot(h, w_ref[...], preferred_element_type=F32)
    o_ref[...] = res.astype(BF16)
    f_ref[...] = res[:, 512:1024]


def _log_sigmoid(z):
    return jnp.minimum(z, 0.0) - jnp.log1p(jnp.exp(-jnp.abs(z)))


def _inproj_odd_body(h, w_refs, out_refs):
    (w_ref, wa_ref, wa2_ref, ba2_ref), (o_ref, la_ref) = w_refs, out_refs
    o_ref[...] = jnp.dot(h, w_ref[...], preferred_element_type=F32).astype(BF16)
    ga = jnp.dot(h, wa_ref[...], preferred_element_type=F32)
    z = jnp.dot(ga.astype(BF16), wa2_ref[...], preferred_element_type=F32) + ba2_ref[...]
    la_ref[...] = _log_sigmoid(z) * (1.0 / GLA_GATE_NORM)


def _inproj_kernel(x_ref, mod_ref, nw_ref, *refs, body, n_w):
    h = _norm_mod(x_ref[...], nw_ref[...], mod_ref, 0, 1).astype(BF16)
    body(h, refs[:n_w], refs[n_w:])


def _inproj_out(t, tm, n_main, index_map):
    return ([pl.BlockSpec((tm, n_main), index_map), pl.BlockSpec((tm, 512), index_map)],
            [jax.ShapeDtypeStruct((t, n_main), BF16), jax.ShapeDtypeStruct((t, 512), F32)])


def _inproj(x2, mod4, layer, seq, nw, weights, body):
    t, d = x2.shape
    tm = min(INPROJ_ROW_TILE, seq)
    per_b = seq // tm
    out_specs, out_shape = _inproj_out(t, tm, weights[0].shape[1], lambda i: (i, 0))
    return pl.pallas_call(
        functools.partial(_inproj_kernel, body=body, n_w=len(weights)),
        grid=(t // tm,),
        in_specs=[pl.BlockSpec((tm, d), lambda i: (i, 0)),
                  pl.BlockSpec((None, None, 6, d), lambda i: (layer, i // per_b, 0, 0)),
                  pl.BlockSpec((1, d), lambda i: (0, 0))]
                 + [pl.BlockSpec(w.shape, lambda i: (0, 0)) for w in weights],
        out_specs=out_specs,
        out_shape=out_shape,
        compiler_params=_cparams(("parallel",)),
        name=f"inproj_l{layer}",
    )(x2, mod4, nw.reshape(1, d), *weights)


def _causal_mask():
    r = lax.broadcasted_iota(jnp.int32, (CHUNK, CHUNK), 0)
    c = lax.broadcasted_iota(jnp.int32, (CHUNK, CHUNK), 1)
    return r >= c


def _dot_nt(a, b):
    return lax.dot_general(a, b, (((1,), (1,)), ((), ())), preferred_element_type=F32)


def _dot_tn(a, b):
    return lax.dot_general(a, b, (((0,), (0,)), ((), ())), preferred_element_type=F32)


def _head_out(o, g, nw, n_norm):
    ms = jnp.sum(o * o, axis=-1, keepdims=True) * (1.0 / n_norm)
    return o * lax.rsqrt(ms + EPS) * nw * _silu(g)


def _cumsum_rows(lg):
    tril = jnp.where(_causal_mask(), 1.0, 0.0).astype(BF16)
    hi = lg.astype(BF16)
    lo = (lg - hi.astype(F32)).astype(BF16)
    return jnp.dot(tril, hi, preferred_element_type=F32) + jnp.dot(tril, lo, preferred_element_type=F32)


def _decay_heads(q, k, v, b, st_ref, st_base, dv):
    b_mid = b[CHUNK // 2 - 1:CHUNK // 2, :]
    b_last = b[CHUNK - 1:CHUNK, :]
    qe = q * jnp.exp(b - b_mid)
    ke = k * jnp.exp(b_mid - b)
    qs = qe.astype(BF16)
    ks = ke.astype(BF16)
    qi = (qe * jnp.exp(b_mid)).astype(BF16)
    kd = (ke * jnp.exp(b_last - b_mid)).astype(BF16)
    dec = jnp.exp(b_last)
    mask = _causal_mask()
    outs = []
    for h in range(N_HEADS):
        kl = slice(h * HEAD_DIM, (h + 1) * HEAD_DIM)
        vh = v[:, h * dv:(h + 1) * dv]
        scores = jnp.where(mask, _dot_nt(qs[:, kl], ks[:, kl]), 0.0)
        o = jnp.dot(scores.astype(BF16), vh, preferred_element_type=F32)
        st = st_ref[st_base + h]
        o = o + _dot_nt(qi[:, kl], st.astype(BF16))
        st_ref[st_base + h] = dec[:, kl] * st + _dot_tn(vh, kd[:, kl])
        outs.append(o)
    return outs


def _retention_log_decay():
    headf = (lax.broadcasted_iota(jnp.int32, (1, N_HEADS * HEAD_DIM), 1) // HEAD_DIM).astype(F32)
    return jnp.log1p(-jnp.exp2(-5.0 - headf))


def _even_mixer_kernel(proj_ref, hf_ref, lbl_ref, hgnw_ref, retnw_ref, cos_ref, sin_ref, o_ref,
                       st_ref, eq_ref, ek_ref, dm_ref, *, layer, n_chunks, ret_chunk):
    hd, w = HEAD_DIM, N_HEADS * HEAD_DIM
    lg = _retention_log_decay()

    @pl.when(jnp.logical_and(pl.program_id(0) == 0, pl.program_id(1) == 0))
    def _():
        pos = lax.broadcasted_iota(jnp.int32, (ret_chunk, w), 0).astype(F32)
        eq_ref[...] = jnp.exp((pos + 1.0) * lg)
        ek_ref[...] = jnp.exp((ret_chunk - 1.0 - pos) * lg)
        r = lax.broadcasted_iota(jnp.int32, (ret_chunk, ret_chunk), 0)
        cc = lax.broadcasted_iota(jnp.int32, (ret_chunk, ret_chunk), 1)
        dt = (r - cc).astype(F32)
        for h in range(N_HEADS):
            dm_ref[h] = jnp.where(r >= cc, jnp.exp(dt * lg[:, h * hd:h * hd + 1]), 0.0)

    @pl.when(pl.program_id(1) == 0)
    def _():
        st_ref[...] = jnp.zeros_like(st_ref)

    lbl = lbl_ref[...]
    e = jnp.exp(lbl - jnp.max(lbl, axis=0, keepdims=True))
    p = e / jnp.sum(e, axis=0, keepdims=True)
    lb = jnp.sum(p[:layer + 1], axis=0, keepdims=True)
    hgnw = hgnw_ref[...]
    retnw = retnw_ref[...]
    e_c = jnp.exp(ret_chunk * lg)

    def rot(xx, cs, sn):
        return xx * cs + pltpu.roll(xx, hd // 2, 1) * sn

    for c in range(n_chunks):
        sl = pl.ds(c * CHUNK, CHUNK)
        f = lb + (1.0 - lb) * _sigmoid(hf_ref[sl, :])
        b = _cumsum_rows(jnp.log(f))
        q = _sigmoid(proj_ref[sl, 0:w].astype(F32))
        outs = _decay_heads(q, 1.0 - f, proj_ref[sl, 2 * w:3 * w], b, st_ref, 0, hd)
        for h, o in enumerate(outs):
            cl = slice(h * hd, (h + 1) * hd)
            g = proj_ref[sl, 3 * w + h * hd:3 * w + (h + 1) * hd].astype(F32)
            o_ref[sl, cl] = _head_out(o, g, hgnw[:, cl], hd).astype(BF16)

    for c in range(n_chunks * CHUNK // ret_chunk):
        sl = pl.ds(c * ret_chunk, ret_chunk)
        cs = cos_ref[sl, :]
        sn = sin_ref[sl, :]
        for h in range(N_HEADS):
            cl = slice(h * hd, (h + 1) * hd)
            qh = rot(proj_ref[sl, 4 * w + h * hd:4 * w + (h + 1) * hd].astype(F32), cs, sn)
            kh = rot(proj_ref[sl, 5 * w + h * hd:5 * w + (h + 1) * hd].astype(F32), cs, sn) * (hd ** -0.5)
            vh = proj_ref[sl, 6 * w + h * hd:6 * w + (h + 1) * hd]
            scores = _dot_nt(qh.astype(BF16), kh.astype(BF16)) * dm_ref[h]
            o = jnp.dot(scores.astype(BF16), vh, preferred_element_type=F32)
            st = st_ref[N_HEADS + h]
            o = o + _dot_nt((qh * eq_ref[:, cl]).astype(BF16), st.astype(BF16))
            st_ref[N_HEADS + h] = e_c[:, cl] * st + _dot_tn(vh, (kh * ek_ref[:, cl]).astype(BF16))
            g = proj_ref[sl, 7 * w + h * hd:7 * w + (h + 1) * hd].astype(F32)
            o_ref[sl, w + h * hd:w + (h + 1) * hd] = _head_out(o, g, retnw[:, cl], hd).astype(BF16)


def _gla_mixer_kernel(proj_ref, la_ref, nw_ref, o_ref, st_ref, *, n_chunks):
    hd, w = HEAD_DIM, N_HEADS * HEAD_DIM
    dv = 2 * hd

    @pl.when(pl.program_id(1) == 0)
    def _():
        st_ref[...] = jnp.zeros_like(st_ref)

    nw = nw_ref[...]
    for c in range(n_chunks):
        sl = pl.ds(c * CHUNK, CHUNK)
        b = _cumsum_rows(la_ref[sl, :])
        q = proj_ref[sl, 0:w].astype(F32) * (hd ** -0.5)
        k = proj_ref[sl, w:2 * w].astype(F32)
        outs = _decay_heads(q, k, proj_ref[sl, 2 * w:2 * w + N_HEADS * dv], b, st_ref, 0, dv)
        for h, o in enumerate(outs):
            cl = slice(h * dv, (h + 1) * dv)
            g = proj_ref[sl, 4 * w + h * dv:4 * w + (h + 1) * dv].astype(F32)
            o_ref[sl, cl] = _head_out(o, g, nw[:, cl], dv).astype(BF16)


def _mixer_outproj_kernel(*refs, mixer, n_in):
    mix_in = refs[:n_in]
    outproj_in = refs[n_in:n_in + 6]
    outproj_out = refs[n_in + 6:n_in + 10]
    o_tile, *mix_scratch = refs[n_in + 10:]
    mixer(*mix_in, o_tile, *mix_scratch)
    _outproj_kernel(o_tile, *outproj_in, *outproj_out)


def _mixer_call(mixer, bsz, seq, ts, in_specs, scratch_shapes, name, args, outproj):
    w_out, x2, mod4, layer, ffn_nw, wr_t, br_col = outproj
    t, d = x2.shape
    per_b = seq // ts
    d_out = N_HEADS * 2 * HEAD_DIM
    full = lambda a: pl.BlockSpec(a.shape, lambda b, s: (0,) * a.ndim)
    rows = lambda width: pl.BlockSpec((ts, width), lambda b, s: (b * per_b + s, 0))
    tile3 = lambda n_rows, width: pl.BlockSpec((None, n_rows, width), lambda b, s: (b * per_b + s, 0, 0))
    ffn_nw = ffn_nw.reshape(1, d)
    return pl.pallas_call(
        functools.partial(_mixer_outproj_kernel, mixer=mixer, n_in=len(in_specs)),
        grid=(bsz, per_b),
        in_specs=in_specs + [full(w_out), rows(d),
                             pl.BlockSpec((None, None, 6, d), lambda b, s: (layer, b, 0, 0)),
                             full(ffn_nw), full(wr_t), full(br_col)],
        out_specs=[rows(d), rows(d), tile3(SUBLANES, ts), tile3(N_EXPERTS, LANES)],
        out_shape=[jax.ShapeDtypeStruct((t, d), F32),
                   jax.ShapeDtypeStruct((t, d), BF16),
                   jax.ShapeDtypeStruct((t // ts, SUBLANES, ts), F32),
                   jax.ShapeDtypeStruct((t // ts, N_EXPERTS, LANES), F32)],
        scratch_shapes=[pltpu.VMEM((ts, d_out), BF16)] + scratch_shapes,
        compiler_params=_cparams(("arbitrary", "arbitrary")),
        name=name,
    )(*args, w_out, x2, mod4, ffn_nw, wr_t, br_col)


def _even_mixers(proj, hf, lb_logits, hg_nw, ret_nw, bsz, seq, layer, outproj):
    ts = min(MIX_SEQ_TILE, seq)
    rc = min(RET_CHUNK, ts)
    per_b = seq // ts
    hd, w = HEAD_DIM, N_HEADS * HEAD_DIM
    rows = lambda width: pl.BlockSpec((ts, width), lambda b, s: (b * per_b + s, 0))
    full = lambda a: pl.BlockSpec(a.shape, lambda b, s: (0,) * a.ndim)
    inv = ROPE_BASE ** (-jnp.arange(0, hd, 2, dtype=F32) / hd)
    ang = jnp.arange(seq, dtype=F32)[:, None] * inv[None, :]
    cos2 = jnp.concatenate([jnp.cos(ang), jnp.cos(ang)], axis=-1)
    sin2 = jnp.concatenate([-jnp.sin(ang), jnp.sin(ang)], axis=-1)
    tab = pl.BlockSpec((ts, hd), lambda b, s: (s, 0))
    hg_nw, ret_nw = hg_nw.reshape(1, -1), ret_nw.reshape(1, -1)
    scratch = [pltpu.VMEM((2 * N_HEADS, hd, hd), F32), pltpu.VMEM((rc, w), F32), pltpu.VMEM((rc, w), F32),
               pltpu.VMEM((N_HEADS, rc, rc), F32)]
    return _mixer_call(
        functools.partial(_even_mixer_kernel, layer=layer, n_chunks=ts // CHUNK, ret_chunk=rc), bsz, seq, ts,
        [rows(proj.shape[1]), rows(hf.shape[1]), full(lb_logits), full(hg_nw), full(ret_nw), tab, tab],
        scratch, f"even_mixer_l{layer}",
        (proj, hf, lb_logits, hg_nw, ret_nw, cos2, sin2), outproj)


def _odd_mixer(proj, log_a, gla_nw, bsz, seq, layer, outproj):
    ts = min(MIX_SEQ_TILE, seq)
    per_b = seq // ts
    rows = lambda width: pl.BlockSpec((ts, width), lambda b, s: (b * per_b + s, 0))
    gla_nw = gla_nw.reshape(1, -1)
    return _mixer_call(
        functools.partial(_gla_mixer_kernel, n_chunks=ts // CHUNK), bsz, seq, ts,
        [rows(proj.shape[1]), rows(log_a.shape[1]), pl.BlockSpec(gla_nw.shape, lambda b, s: (0, 0))],
        [pltpu.VMEM((N_HEADS, 2 * HEAD_DIM, HEAD_DIM), F32)], f"gla_mixer_l{layer}",
        (proj, log_a, gla_nw), outproj)


def _row_pick(n_rows, rows):
    rid = lax.broadcasted_iota(jnp.int32, (n_rows, rows[0].shape[1]), 0)
    out = jnp.zeros(rid.shape, F32)
    for j, row in reversed(list(enumerate(rows))):
        out = jnp.where(rid == j, row, out)
    return out


def _route(h2, wr_t, br):
    neg = -jnp.inf
    logits = _dot_nt(wr_t, h2)
    tm = logits.shape[1]
    eid = lax.broadcasted_iota(jnp.int32, logits.shape, 0)
    ex = jnp.exp(logits - jnp.max(logits, axis=0, keepdims=True))
    probs = ex / jnp.sum(ex, axis=0, keepdims=True)
    sel = probs + br
    grp = eid // EXPERTS_PER_GROUP
    gs = [jnp.max(jnp.where(grp == g, sel, neg), axis=0, keepdims=True) for g in range(N_GROUPS)]
    best = functools.reduce(jnp.maximum, gs)
    gidx = jnp.full(best.shape, N_GROUPS - 1, jnp.int32)
    for g in range(N_GROUPS - 2, -1, -1):
        gidx = jnp.where(gs[g] == best, g, gidx)
    masked = jnp.where(grp == gidx, sel, neg)
    m1 = jnp.max(masked, axis=0, keepdims=True)
    i1 = jnp.min(jnp.where(masked == m1, eid, N_EXPERTS), axis=0, keepdims=True)
    masked2 = jnp.where(eid == i1, neg, masked)
    m2 = jnp.max(masked2, axis=0, keepdims=True)
    i2 = jnp.min(jnp.where(masked2 == m2, eid, N_EXPERTS), axis=0, keepdims=True)
    p1 = jnp.sum(jnp.where(eid == i1, probs, 0.0), axis=0, keepdims=True)
    p2 = jnp.sum(jnp.where(eid == i2, probs, 0.0), axis=0, keepdims=True)
    den = p1 + p2
    chosen = jnp.where(eid == i1, 1.0, jnp.where(eid == i2, 1.0, 0.0))
    r = lax.broadcasted_iota(jnp.int32, (tm, tm), 0)
    c = lax.broadcasted_iota(jnp.int32, (tm, tm), 1)
    earlier = jnp.where(r < c, 1.0, 0.0).astype(BF16)
    rank = jnp.dot(chosen.astype(BF16), earlier, preferred_element_type=F32)
    rank1 = jnp.sum(jnp.where(eid == i1, rank, 0.0), axis=0, keepdims=True)
    rank2 = jnp.sum(jnp.where(eid == i2, rank, 0.0), axis=0, keepdims=True)
    meta = _row_pick(SUBLANES, [i1.astype(F32), i2.astype(F32), p1 / den, p2 / den, rank1, rank2])
    return meta, jnp.sum(chosen, axis=1, keepdims=True)


def _outproj_kernel(o_ref, w_ref, x_ref, mod_ref, nw_ref, wr_ref, br_ref, x1_ref, h2_ref, meta_ref, cnt_ref):
    y = jnp.dot(o_ref[...], w_ref[...], preferred_element_type=F32)
    x1 = x_ref[...] + mod_ref[2:3, :] * y
    x1_ref[...] = x1
    h2 = _norm_mod(x1, nw_ref[...], mod_ref, 3, 4).astype(BF16)
    h2_ref[...] = h2
    meta, counts = _route(h2, wr_ref[...], br_ref[...])
    meta_ref[...] = meta
    cnt_ref[...] = jnp.broadcast_to(counts, cnt_ref.shape)


class _MoePlan:
    def __init__(self, t, tb):
        self.tb = tb
        self.n_tiles = t // tb
        self.local_rows = -(-(TOP_K * tb + N_EXPERTS * (ROW_GROUP - 1)) // PERM_BLOCK) * PERM_BLOCK
        self.ffn_tile = min(FFN_ROW_TILE, TOP_K * tb)
        worst_rows = TOP_K * t + self.n_tiles * N_EXPERTS * (ROW_GROUP - 1)
        self.ffn_tiles = -(-worst_rows // self.ffn_tile) + N_EXPERTS
        self.sorted_rows = self.ffn_tiles * self.ffn_tile


def _moe_tables(counts, plan):
    i32 = jnp.int32
    n_pad = (counts + (ROW_GROUP - 1)) // ROW_GROUP * ROW_GROUP
    ends = jnp.cumsum(n_pad, axis=1)
    loc = ends - n_pad
    n_e = jnp.sum(n_pad, axis=0)
    n_e_pad = -(-n_e // plan.ffn_tile) * plan.ffn_tile
    region_end = jnp.cumsum(n_e_pad)
    base = region_end - n_e_pad
    goff = base[None, :] + jnp.cumsum(n_pad, axis=0) - n_pad
    n_used = region_end[-1] // plan.ffn_tile
    tile_row = jnp.minimum(jnp.arange(plan.ffn_tiles, dtype=i32), n_used - 1) * plan.ffn_tile
    tile_expert = jnp.minimum(jnp.sum(region_end[None, :] <= tile_row[:, None], axis=-1), N_EXPERTS - 1)
    flat = lambda a: a.reshape(-1).astype(i32)
    return dict(loc=loc, seg_src=flat(loc), seg_dst=flat(goff), seg_rows=flat(n_pad),
                tail_start=(base + n_e).astype(i32), tail_rows=(n_e_pad - n_e).astype(i32),
                tile_expert=tile_expert.astype(i32), n_used=n_used.reshape(1).astype(i32))


def _segment_copy(buf_ref, slot, src_row, hbm_ref, dst_row, n_rows, sem, to_hbm):
    al = lambda v: pl.multiple_of(v, ROW_GROUP)
    local = buf_ref.at[slot, pl.ds(al(src_row), al(n_rows))]
    remote = hbm_ref.at[pl.ds(al(dst_row), al(n_rows))]
    src, dst = (local, remote) if to_hbm else (remote, local)
    return pltpu.make_async_copy(src, dst, sem.at[slot])


def _tile_segments(tile, slot, buf_ref, hbm_ref, sem, src_ref, dst_ref, rows_ref, to_hbm, start):
    def seg(e, c):
        k = tile * N_EXPERTS + e
        n = rows_ref[k]

        @pl.when(n > 0)
        def _():
            cp = _segment_copy(buf_ref, slot, src_ref[k], hbm_ref, dst_ref[k], n, sem, to_hbm)
            cp.start() if start else cp.wait()
        return c

    lax.fori_loop(0, N_EXPERTS, seg, 0)


def _dispatch_kernel(src_ref, dst_ref, rows_ref, tstart_ref, trows_ref, nu_ref, h_ref, meta_ref, loc_ref,
                     sorted_ref, cmeta_ref, buf_ref, zero_ref, sem, zsem, *, plan):
    i = pl.program_id(0)
    last = pl.num_programs(0) - 1
    slot = i % 2
    tb, rows_l = plan.tb, plan.local_rows

    def tails(start):
        def tail(e, c):
            n = trows_ref[e]

            @pl.when(n > 0)
            def _():
                cp = pltpu.make_async_copy(zero_ref.at[pl.ds(0, pl.multiple_of(n, ROW_GROUP))],
                                           sorted_ref.at[pl.ds(pl.multiple_of(tstart_ref[e], ROW_GROUP),
                                                               pl.multiple_of(n, ROW_GROUP))], zsem.at[0])
                cp.start() if start else cp.wait()
            return c

        lax.fori_loop(0, N_EXPERTS, tail, 0)

    def spare_copy(j):
        row = j * plan.ffn_tile if isinstance(j, int) else pl.multiple_of(j * plan.ffn_tile, plan.ffn_tile)
        return pltpu.make_async_copy(zero_ref, sorted_ref.at[pl.ds(row, plan.ffn_tile)], zsem.at[1])

    @pl.when(i == 0)
    def _():
        zero_ref[...] = jnp.zeros_like(zero_ref)
        tails(True)
        lax.fori_loop(nu_ref[0], plan.ffn_tiles, lambda j, c: (spare_copy(j).start(), c)[1], 0)

    def wait_tile(tile, s):
        _tile_segments(tile, s, buf_ref, sorted_ref, sem, src_ref, dst_ref, rows_ref, True, False)

    @pl.when(i >= 2)
    def _():
        wait_tile(i - 2, slot)

    meta = meta_ref[...]
    eid = lax.broadcasted_iota(jnp.int32, (N_EXPERTS, tb), 0).astype(F32)
    loc_col = loc_ref[...]

    def local_row(e_row, rank_row):
        return jnp.sum(jnp.where(eid == e_row, loc_col, 0.0), axis=0, keepdims=True) + rank_row

    r1_row = local_row(meta[0:1, :], meta[4:5, :])
    r2_row = local_row(meta[1:2, :], meta[5:6, :])
    r1h, r2h = jnp.floor(r1_row * (1.0 / 256.0)), jnp.floor(r2_row * (1.0 / 256.0))
    pieces = _row_pick(LANES, [r1h, r1_row - 256.0 * r1h, r2h, r2_row - 256.0 * r2h,
                               meta[2:3, :], meta[3:4, :]]).astype(BF16)
    eye = jnp.where(lax.broadcasted_iota(jnp.int32, (tb, tb), 0) == lax.broadcasted_iota(jnp.int32, (tb, tb), 1),
                    1.0, 0.0).astype(BF16)
    cmeta_ref[...] = _dot_nt(eye, pieces)
    h = h_ref[...]
    blk = PERM_BLOCK
    for rb in range(rows_l // blk):
        rid = (lax.broadcasted_iota(jnp.int32, (blk, tb), 0) + rb * blk).astype(F32)
        perm = jnp.where(rid == r1_row, 1.0, jnp.where(rid == r2_row, 1.0, 0.0)).astype(BF16)
        buf_ref[slot, rb * blk:(rb + 1) * blk, :] = jnp.dot(perm, h, preferred_element_type=F32).astype(BF16)

    _tile_segments(i, slot, buf_ref, sorted_ref, sem, src_ref, dst_ref, rows_ref, True, True)

    @pl.when(i == last)
    def _():
        @pl.when(i >= 1)
        def _():
            wait_tile(i - 1, 1 - slot)
        wait_tile(i, slot)
        tails(False)
        lax.fori_loop(nu_ref[0], plan.ffn_tiles, lambda j, c: (spare_copy(0).wait(), c)[1], 0)


def _dispatch(h2, meta, tabs, plan):
    t, d = h2.shape
    loc_col = tabs["loc"].astype(F32)[:, :, None]
    grid_spec = pltpu.PrefetchScalarGridSpec(
        num_scalar_prefetch=6,
        grid=(plan.n_tiles,),
        in_specs=[pl.BlockSpec((plan.tb, d), lambda i, *_: (i, 0)),
                  pl.BlockSpec((None, SUBLANES, plan.tb), lambda i, *_: (i, 0, 0)),
                  pl.BlockSpec((None, N_EXPERTS, 1), lambda i, *_: (i, 0, 0))],
        out_specs=[pl.BlockSpec(memory_space=pl.ANY),
                   pl.BlockSpec((plan.tb, LANES), lambda i, *_: (i, 0))],
        scratch_shapes=[pltpu.VMEM((2, plan.local_rows, d), BF16),
                        pltpu.VMEM((plan.ffn_tile, d), BF16),
                        pltpu.SemaphoreType.DMA((2,)),
                        pltpu.SemaphoreType.DMA((2,))])
    return pl.pallas_call(
        functools.partial(_dispatch_kernel, plan=plan),
        grid_spec=grid_spec,
        out_shape=[jax.ShapeDtypeStruct((plan.sorted_rows, d), BF16),
                   jax.ShapeDtypeStruct((t, LANES), F32)],
        compiler_params=_cparams(("arbitrary",)),
        name="moe_dispatch",
    )(tabs["seg_src"], tabs["seg_dst"], tabs["seg_rows"], tabs["tail_start"], tabs["tail_rows"], tabs["n_used"],
      h2, meta, loc_col)


def _ffn_kernel(te_ref, nu_ref, x_ref, wg_ref, wu_ref, wd_ref, y_ref, wg_s, wu_s, wd_s):
    j = pl.program_id(0)
    used = j < nu_ref[0]
    new_expert = jnp.logical_or(j == 0, te_ref[j] != te_ref[jnp.maximum(j - 1, 0)])

    @pl.when(jnp.logical_and(used, new_expert))
    def _():
        wg_s[...] = wg_ref[0].astype(BF16)
        wu_s[...] = wu_ref[0].astype(BF16)
        wd_s[...] = wd_ref[0].astype(BF16)

    @pl.when(used)
    def _():
        x = x_ref[...]
        g = jnp.dot(x, wg_s[...], preferred_element_type=F32)
        u = jnp.dot(x, wu_s[...], preferred_element_type=F32)
        y_ref[...] = jnp.dot((_silu(g) * u).astype(BF16), wd_s[...], preferred_element_type=F32).astype(BF16)

    @pl.when(jnp.logical_not(used))
    def _():
        y_ref[...] = jnp.zeros_like(y_ref)


def _ffn(xs, wg, wu, wd, layer, tabs, plan):
    d, de = wg.shape[2], wg.shape[3]
    tm = plan.ffn_tile
    row_blk = lambda j, te, nu: (jnp.minimum(j, nu[0] - 1), 0)
    grid_spec = pltpu.PrefetchScalarGridSpec(
        num_scalar_prefetch=2,
        grid=(plan.ffn_tiles,),
        in_specs=[pl.BlockSpec((tm, d), row_blk),
                  pl.BlockSpec((None, 1, d, de), lambda j, te, nu: (layer, te[j], 0, 0)),
                  pl.BlockSpec((None, 1, d, de), lambda j, te, nu: (layer, te[j], 0, 0)),
                  pl.BlockSpec((None, 1, de, d), lambda j, te, nu: (layer, te[j], 0, 0))],
        out_specs=pl.BlockSpec((tm, d), lambda j, te, nu: (j, 0)),
        scratch_shapes=[pltpu.VMEM((d, de), BF16), pltpu.VMEM((d, de), BF16), pltpu.VMEM((de, d), BF16)])
    return pl.pallas_call(
        _ffn_kernel,
        grid_spec=grid_spec,
        out_shape=jax.ShapeDtypeStruct(xs.shape, BF16),
        compiler_params=_cparams(("arbitrary",)),
        name="moe_ffn",
    )(tabs["tile_expert"], tabs["n_used"], xs, wg, wu, wd)


def _combine_kernel(src_ref, dst_ref, rows_ref, ys_ref, cm0_ref, cmn_ref, x_ref, mod_ref, fw_ref, *refs,
                    plan, final, next_body, n_w):
    if next_body is not None:
        modn_ref, nwn_ref = refs[:2]
        wn_refs = refs[2:2 + n_w]
        o_ref, *next_out = refs[2 + n_w:5 + n_w]
        buf_ref, comb_a, comb_b, sem = refs[5 + n_w:]
    else:
        o_ref, buf_ref, comb_a, comb_b, sem = refs
    comb_refs = (comb_a, comb_b)
    i = pl.program_id(0)
    n_tiles = pl.num_programs(0)
    slot = i % 2

    def fetch(tile, s, start):
        _tile_segments(tile, s, buf_ref, ys_ref, sem, src_ref, dst_ref, rows_ref, False, start)

    @pl.when(i == 0)
    def _():
        buf_ref[...] = jnp.zeros_like(buf_ref)
        fetch(0, 0, True)

    @pl.when(i + 1 < n_tiles)
    def _():
        fetch(i + 1, 1 - slot, True)

    fetch(i, slot, False)

    def build(cm_ref, comb_ref):
        cm = cm_ref[...]
        r1 = cm[:, 0:1] * 256.0 + cm[:, 1:2]
        r2 = cm[:, 2:3] * 256.0 + cm[:, 3:4]
        rid = lax.broadcasted_iota(jnp.int32, (plan.tb, plan.local_rows), 1).astype(F32)
        comb_ref[...] = (jnp.where(rid == r1, cm[:, 4:5], 0.0)
                         + jnp.where(rid == r2, cm[:, 5:6], 0.0)).astype(BF16)

    @pl.when(i == 0)
    def _():
        build(cm0_ref, comb_refs[0])

    def step(parity):
        build(cmn_ref, comb_refs[1 - parity])
        moe = jnp.dot(comb_refs[parity][...], buf_ref[parity], preferred_element_type=F32)
        xn = x_ref[...] + mod_ref[5:6, :] * moe
        if final:
            ms = jnp.mean(xn * xn, axis=-1, keepdims=True)
            xn = xn * lax.rsqrt(ms + EPS) * fw_ref[...]
        o_ref[...] = xn
        if next_body is not None:
            next_body(_norm_mod(xn, nwn_ref[...], modn_ref, 0, 1).astype(BF16), wn_refs, next_out)

    for parity in range(2):
        pl.when(slot == parity)(functools.partial(step, parity))


def _combine(ys, cmeta, x1, mod4, layer, seq, tabs, plan, final_w, final, next_proj=None):
    t, d = x1.shape
    per_b = seq // plan.tb
    rows = lambda width: pl.BlockSpec((plan.tb, width), lambda i, *_: (i, 0))
    in_specs = [pl.BlockSpec(memory_space=pl.ANY),
                pl.BlockSpec((plan.tb, LANES), lambda i, *_: (0, 0)),
                pl.BlockSpec((plan.tb, LANES), lambda i, *_: (jnp.minimum(i + 1, plan.n_tiles - 1), 0)),
                rows(d),
                pl.BlockSpec((None, None, 6, d), lambda i, *_: (layer, i // per_b, 0, 0)),
                pl.BlockSpec((1, d), lambda i, *_: (0, 0))]
    args = [ys, cmeta, cmeta, x1, mod4, final_w.reshape(1, d)]
    out_specs, out_shape = [rows(d)], [jax.ShapeDtypeStruct((t, d), F32)]
    body, n_w = None, 0
    if next_proj is not None:
        nw_next, weights, body = next_proj
        n_w = len(weights)
        in_specs += ([pl.BlockSpec((None, None, 6, d), lambda i, *_: (layer + 1, i // per_b, 0, 0)),
                      pl.BlockSpec((1, d), lambda i, *_: (0, 0))]
                     + [pl.BlockSpec(w.shape, lambda i, *_: (0, 0)) for w in weights])
        args += [mod4, nw_next.reshape(1, d)] + list(weights)
        proj_specs, proj_shape = _inproj_out(t, plan.tb, weights[0].shape[1], lambda i, *_: (i, 0))
        out_specs, out_shape = out_specs + proj_specs, out_shape + proj_shape
    grid_spec = pltpu.PrefetchScalarGridSpec(
        num_scalar_prefetch=3,
        grid=(plan.n_tiles,),
        in_specs=in_specs,
        out_specs=out_specs,
        scratch_shapes=[pltpu.VMEM((2, plan.local_rows, d), BF16),
                        pltpu.VMEM((plan.tb, plan.local_rows), BF16),
                        pltpu.VMEM((plan.tb, plan.local_rows), BF16),
                        pltpu.SemaphoreType.DMA((2,))])
    return pl.pallas_call(
        functools.partial(_combine_kernel, plan=plan, final=final, next_body=body, n_w=n_w),
        grid_spec=grid_spec,
        out_shape=out_shape,
        compiler_params=_cparams(("arbitrary",)),
        name="moe_combine",
    )(tabs["seg_src"], tabs["seg_dst"], tabs["seg_rows"], *args)


def kernel(x, c, w_ada, b_ada, norm_mix_w, norm_ffn_w, w_in_even, hg_lb_logits, hg_norm_w, ret_norm_w,
           w_out_even, w_in_odd, w_gla_a2, b_gla_a2, gla_norm_w, w_out_odd, w_router, b_router,
           w_gate, w_up, w_down, final_norm_w):
    bsz, seq, d = x.shape
    depth = w_ada.shape[0]
    x2 = x.reshape(bsz * seq, d)
    mod4 = _adaln_mod(c, w_ada, b_ada).reshape(depth, bsz, 6, d)
    wr_t = w_router.T.astype(BF16)
    br_col = b_router.reshape(N_EXPERTS, 1)
    plan = _MoePlan(bsz * seq, min(MOE_TOKEN_TILE, seq))

    def inproj_args(l):
        j = l // 2
        if l % 2 == 0:
            return norm_mix_w[l], [w_in_even[j].astype(BF16)], _inproj_even_body
        n_main = w_in_odd.shape[-1] - GLA_RANK
        wi = w_in_odd[j]
        wa = jnp.pad(wi[:, n_main:], ((0, 0), (0, LANES - GLA_RANK))).astype(BF16)
        wa2 = jnp.pad(w_gla_a2[j], ((0, LANES - GLA_RANK), (0, 0))).astype(BF16)
        return norm_mix_w[l], [wi[:, :n_main].astype(BF16), wa, wa2, b_gla_a2[j].reshape(1, -1)], _inproj_odd_body

    fuse_next = plan.tb == min(INPROJ_ROW_TILE, seq)
    projected = None
    for l in range(depth):
        j = l // 2
        proj, aux = projected if projected is not None else _inproj(x2, mod4, l, seq, *inproj_args(l))
        w_out = (w_out_even if l % 2 == 0 else w_out_odd)[j].astype(BF16)
        outproj = (w_out, x2, mod4, l, norm_ffn_w[l], wr_t, br_col)
        if l % 2 == 0:
            routed = _even_mixers(proj, aux, hg_lb_logits, hg_norm_w[j], ret_norm_w[j], bsz, seq, l, outproj)
        else:
            routed = _odd_mixer(proj, aux, gla_norm_w[j], bsz, seq, l, outproj)
        x1, h2, meta, counts = routed
        tabs = _moe_tables(counts[:, :, 0].astype(jnp.int32), plan)
        xs, cmeta = _dispatch(h2, meta, tabs, plan)
        ys = _ffn(xs, w_gate, w_up, w_down, l, tabs, plan)
        last = l == depth - 1
        outs = _combine(ys, cmeta, x1, mod4, l, seq, tabs, plan, final_norm_w, final=last,
                        next_proj=inproj_args(l + 1) if fuse_next and not last else None)
        x2, projected = outs[0], (tuple(outs[1:]) if len(outs) > 1 else None)
    return x2.reshape(bsz, seq, d)
```

```python
import functools

import jax
import jax.numpy as jnp
from jax import lax
from jax.experimental import pallas as pl
from jax.experimental.pallas import tpu as pltpu

F32 = jnp.float32
BF16 = jnp.bfloat16
HIGHEST = lax.Precision.HIGHEST

EPS = 1e-6
CHUNK = 64
RET_CHUNK = 256
HEAD_DIM = 128
N_HEADS = 4
N_EXPERTS = 16
N_GROUPS = 4
EXPERTS_PER_GROUP = N_EXPERTS // N_GROUPS
GLA_RANK = 16
GLA_GATE_NORM = 16.0
ROPE_BASE = 10000.0
LANES = 128
VMEM_LIMIT = 52 * 1024 * 1024
SUBLANES = 8
INPROJ_ROW_TILE = 512
MIX_SEQ_TILE = 512
TOP_K = 2
ROW_GROUP = 16
MOE_TOKEN_TILE = 512
FFN_ROW_TILE = 1024
PERM_BLOCK = 256
GATHER_RING = 3


def _cparams(sem):
    return pltpu.CompilerParams(dimension_semantics=sem, vmem_limit_bytes=VMEM_LIMIT)


def _sigmoid(z):
    return 0.5 * jnp.tanh(0.5 * z) + 0.5


def _silu(z):
    return z * _sigmoid(z)


def _mod_kernel(c_ref, w_ref, b_ref, o_ref):
    c = c_ref[...]
    o_ref[0] = jnp.dot(_silu(c), w_ref[0], preferred_element_type=F32, precision=HIGHEST) + b_ref[0]


def _adaln_mod(c, w_ada, b_ada):
    depth, d, n = w_ada.shape
    bsz = c.shape[0]
    tn = 1536
    return pl.pallas_call(
        _mod_kernel,
        grid=(depth, n // tn),
        in_specs=[pl.BlockSpec((bsz, d), lambda l, j: (0, 0)),
                  pl.BlockSpec((1, d, tn), lambda l, j: (l, 0, j)),
                  pl.BlockSpec((1, 1, tn), lambda l, j: (l, 0, j))],
        out_specs=pl.BlockSpec((1, bsz, tn), lambda l, j: (l, 0, j)),
        out_shape=jax.ShapeDtypeStruct((depth, bsz, n), F32),
        compiler_params=_cparams(("arbitrary", "arbitrary")),
        name="adaln_mod",
    )(c, w_ada, b_ada.reshape(depth, 1, n))


def _norm_mod(x, nw, mod_ref, shift_row, scale_row):
    ms = jnp.mean(x * x, axis=-1, keepdims=True)
    return x * lax.rsqrt(ms + EPS) * (nw * (1.0 + mod_ref[scale_row:scale_row + 1, :])) + mod_ref[shift_row:shift_row + 1, :]


def _inproj_even_body(h, w_refs, out_refs):
    (w_ref,), (o_ref, f_ref) = w_refs, out_refs
    res = jnp.dot(h, w_ref[...], preferred_element_type=F32)
    o_ref[...] = res.astype(BF16)
    f_ref[...] = res[:, 512:1024]


def _log_sigmoid(z):
    return jnp.minimum(z, 0.0) - jnp.log1p(jnp.exp(-jnp.abs(z)))


def _inproj_odd_body(h, w_refs, out_refs):
    (w_ref, wa_ref, wa2_ref, ba2_ref), (o_ref, la_ref) = w_refs, out_refs
    o_ref[...] = jnp.dot(h, w_ref[...], preferred_element_type=F32).astype(BF16)
    ga = jnp.dot(h, wa_ref[...], preferred_element_type=F32)
    z = jnp.dot(ga.astype(BF16), wa2_ref[...], preferred_element_type=F32) + ba2_ref[...]
    la_ref[...] = _log_sigmoid(z) * (1.0 / GLA_GATE_NORM)


def _inproj_kernel(x_ref, mod_ref, nw_ref, *refs, body, n_w):
    h = _norm_mod(x_ref[...], nw_ref[...], mod_ref, 0, 1).astype(BF16)
    body(h, refs[:n_w], refs[n_w:])


def _inproj_out(t, tm, n_main, index_map):
    return ([pl.BlockSpec((tm, n_main), index_map), pl.BlockSpec((tm, 512), index_map)],
            [jax.ShapeDtypeStruct((t, n_main), BF16), jax.ShapeDtypeStruct((t, 512), F32)])


def _inproj(x2, mod4, layer, seq, nw, weights, body):
    t, d = x2.shape
    tm = min(INPROJ_ROW_TILE, seq)
    per_b = seq // tm
    out_specs, out_shape = _inproj_out(t, tm, weights[0].shape[1], lambda i: (i, 0))
    return pl.pallas_call(
        functools.partial(_inproj_kernel, body=body, n_w=len(weights)),
        grid=(t // tm,),
        in_specs=[pl.BlockSpec((tm, d), lambda i: (i, 0)),
                  pl.BlockSpec((None, None, 6, d), lambda i: (layer, i // per_b, 0, 0)),
                  pl.BlockSpec((1, d), lambda i: (0, 0))]
                 + [pl.BlockSpec(w.shape, lambda i: (0, 0)) for w in weights],
        out_specs=out_specs,
        out_shape=out_shape,
        compiler_params=_cparams(("parallel",)),
        name=f"inproj_l{layer}",
    )(x2, mod4, nw.reshape(1, d), *weights)


def _causal_mask():
    r = lax.broadcasted_iota(jnp.int32, (CHUNK, CHUNK), 0)
    c = lax.broadcasted_iota(jnp.int32, (CHUNK, CHUNK), 1)
    return r >= c


def _dot_nt(a, b):
    return lax.dot_general(a, b, (((1,), (1,)), ((), ())), preferred_element_type=F32)


def _dot_tn(a, b):
    return lax.dot_general(a, b, (((0,), (0,)), ((), ())), preferred_element_type=F32)


def _head_out(o, g, nw, n_norm):
    ms = jnp.sum(o * o, axis=-1, keepdims=True) * (1.0 / n_norm)
    return o * lax.rsqrt(ms + EPS) * nw * _silu(g)


def _cumsum_rows(lg):
    tril = jnp.where(_causal_mask(), 1.0, 0.0).astype(BF16)
    hi = lg.astype(BF16)
    lo = (lg - hi.astype(F32)).astype(BF16)
    return jnp.dot(tril, hi, preferred_element_type=F32) + jnp.dot(tril, lo, preferred_element_type=F32)


def _decay_heads(q, k, v, b, st_ref, st_base, dv):
    b_mid = b[CHUNK // 2 - 1:CHUNK // 2, :]
    b_last = b[CHUNK - 1:CHUNK, :]
    qe = q * jnp.exp(b - b_mid)
    ke = k * jnp.exp(b_mid - b)
    qs = qe.astype(BF16)
    ks = ke.astype(BF16)
    qi = (qe * jnp.exp(b_mid)).astype(BF16)
    kd = (ke * jnp.exp(b_last - b_mid)).astype(BF16)
    dec = jnp.exp(b_last)
    mask = _causal_mask()
    outs = []
    for h in range(N_HEADS):
        kl = slice(h * HEAD_DIM, (h + 1) * HEAD_DIM)
        vh = v[:, h * dv:(h + 1) * dv]
        scores = jnp.where(mask, _dot_nt(qs[:, kl], ks[:, kl]), 0.0)
        o = jnp.dot(scores.astype(BF16), vh, preferred_element_type=F32)
        st = st_ref[st_base + h]
        o = o + _dot_nt(qi[:, kl], st.astype(BF16))
        st_ref[st_base + h] = dec[:, kl] * st + _dot_tn(vh, kd[:, kl])
        outs.append(o)
    return outs


def _retention_log_decay():
    headf = (lax.broadcasted_iota(jnp.int32, (1, N_HEADS * HEAD_DIM), 1) // HEAD_DIM).astype(F32)
    return jnp.log1p(-jnp.exp2(-5.0 - headf))


def _even_mixer_kernel(proj_ref, hf_ref, lbl_ref, hgnw_ref, retnw_ref, cos_ref, sin_ref, o_ref,
                       st_ref, eq_ref, ek_ref, dm_ref, *, layer, n_chunks, ret_chunk):
    hd, w = HEAD_DIM, N_HEADS * HEAD_DIM
    lg = _retention_log_decay()

    @pl.when(jnp.logical_and(pl.program_id(0) == 0, pl.program_id(1) == 0))
    def _():
        pos = lax.broadcasted_iota(jnp.int32, (ret_chunk, w), 0).astype(F32)
        eq_ref[...] = jnp.exp((pos + 1.0) * lg)
        ek_ref[...] = jnp.exp((ret_chunk - 1.0 - pos) * lg)
        r = lax.broadcasted_iota(jnp.int32, (ret_chunk, ret_chunk), 0)
        cc = lax.broadcasted_iota(jnp.int32, (ret_chunk, ret_chunk), 1)
        dt = (r - cc).astype(F32)
        for h in range(N_HEADS):
            dm_ref[h] = jnp.where(r >= cc, jnp.exp(dt * lg[:, h * hd:h * hd + 1]), 0.0)

    @pl.when(pl.program_id(1) == 0)
    def _():
        st_ref[...] = jnp.zeros_like(st_ref)

    lbl = lbl_ref[...]
    e = jnp.exp(lbl - jnp.max(lbl, axis=0, keepdims=True))
    p = e / jnp.sum(e, axis=0, keepdims=True)
    lb = jnp.sum(p[:layer + 1], axis=0, keepdims=True)
    hgnw = hgnw_ref[...]
    retnw = retnw_ref[...]
    e_c = jnp.exp(ret_chunk * lg)

    def rot(xx, cs, sn):
        return xx * cs + pltpu.roll(xx, hd // 2, 1) * sn

    for c in range(n_chunks):
        sl = pl.ds(c * CHUNK, CHUNK)
        f = lb + (1.0 - lb) * _sigmoid(hf_ref[sl, :])
        b = _cumsum_rows(jnp.log(f))
        q = _sigmoid(proj_ref[sl, 0:w].astype(F32))
        outs = _decay_heads(q, 1.0 - f, proj_ref[sl, 2 * w:3 * w], b, st_ref, 0, hd)
        for h, o in enumerate(outs):
            cl = slice(h * hd, (h + 1) * hd)
            g = proj_ref[sl, 3 * w + h * hd:3 * w + (h + 1) * hd].astype(F32)
            o_ref[sl, cl] = _head_out(o, g, hgnw[:, cl], hd).astype(BF16)

    for c in range(n_chunks * CHUNK // ret_chunk):
        sl = pl.ds(c * ret_chunk, ret_chunk)
        cs = cos_ref[sl, :]
        sn = sin_ref[sl, :]
        for h in range(N_HEADS):
            cl = slice(h * hd, (h + 1) * hd)
            qh = rot(proj_ref[sl, 4 * w + h * hd:4 * w + (h + 1) * hd].astype(F32), cs, sn)
            kh = rot(proj_ref[sl, 5 * w + h * hd:5 * w + (h + 1) * hd].astype(F32), cs, sn) * (hd ** -0.5)
            vh = proj_ref[sl, 6 * w + h * hd:6 * w + (h + 1) * hd]
            scores = _dot_nt(qh.astype(BF16), kh.astype(BF16)) * dm_ref[h]
            o = jnp.dot(scores.astype(BF16), vh, preferred_element_type=F32)
            st = st_ref[N_HEADS + h]
            o = o + _dot_nt((qh * eq_ref[:, cl]).astype(BF16), st.astype(BF16))
            st_ref[N_HEADS + h] = e_c[:, cl] * st + _dot_tn(vh, (kh * ek_ref[:, cl]).astype(BF16))
            g = proj_ref[sl, 7 * w + h * hd:7 * w + (h + 1) * hd].astype(F32)
            o_ref[sl, w + h * hd:w + (h + 1) * hd] = _head_out(o, g, retnw[:, cl], hd).astype(BF16)


def _gla_mixer_kernel(proj_ref, la_ref, nw_ref, o_ref, st_ref, *, n_chunks):
    hd, w = HEAD_DIM, N_HEADS * HEAD_DIM
    dv = 2 * hd

    @pl.when(pl.program_id(1) == 0)
    def _():
        st_ref[...] = jnp.zeros_like(st_ref)

    nw = nw_ref[...]
    for c in range(n_chunks):
        sl = pl.ds(c * CHUNK, CHUNK)
        b = _cumsum_rows(la_ref[sl, :])
        q = proj_ref[sl, 0:w].astype(F32) * (hd ** -0.5)
        k = proj_ref[sl, w:2 * w].astype(F32)
        outs = _decay_heads(q, k, proj_ref[sl, 2 * w:2 * w + N_HEADS * dv], b, st_ref, 0, dv)
        for h, o in enumerate(outs):
            cl = slice(h * dv, (h + 1) * dv)
            g = proj_ref[sl, 4 * w + h * dv:4 * w + (h + 1) * dv].astype(F32)
            o_ref[sl, cl] = _head_out(o, g, nw[:, cl], dv).astype(BF16)


def _mixer_outproj_kernel(*refs, mixer, n_in):
    mix_in = refs[:n_in]
    outproj_in = refs[n_in:n_in + 6]
    outproj_out = refs[n_in + 6:n_in + 10]
    o_tile, *mix_scratch = refs[n_in + 10:]
    mixer(*mix_in, o_tile, *mix_scratch)
    _outproj_kernel(o_tile, *outproj_in, *outproj_out)


def _mixer_call(mixer, bsz, seq, ts, in_specs, scratch_shapes, name, args, outproj):
    w_out, x2, mod4, layer, ffn_nw, wr_t, br_col = outproj
    t, d = x2.shape
    per_b = seq // ts
    d_out = N_HEADS * 2 * HEAD_DIM
    full = lambda a: pl.BlockSpec(a.shape, lambda b, s: (0,) * a.ndim)
    rows = lambda width: pl.BlockSpec((ts, width), lambda b, s: (b * per_b + s, 0))
    tile3 = lambda n_rows, width: pl.BlockSpec((None, n_rows, width), lambda b, s: (b * per_b + s, 0, 0))
    ffn_nw = ffn_nw.reshape(1, d)
    return pl.pallas_call(
        functools.partial(_mixer_outproj_kernel, mixer=mixer, n_in=len(in_specs)),
        grid=(bsz, per_b),
        in_specs=in_specs + [full(w_out), rows(d),
                             pl.BlockSpec((None, None, 6, d), lambda b, s: (layer, b, 0, 0)),
                             full(ffn_nw), full(wr_t), full(br_col)],
        out_specs=[rows(d), rows(d), tile3(SUBLANES, ts), tile3(N_EXPERTS, LANES)],
        out_shape=[jax.ShapeDtypeStruct((t, d), F32),
                   jax.ShapeDtypeStruct((t, d), BF16),
                   jax.ShapeDtypeStruct((t // ts, SUBLANES, ts), F32),
                   jax.ShapeDtypeStruct((t // ts, N_EXPERTS, LANES), F32)],
        scratch_shapes=[pltpu.VMEM((ts, d_out), BF16)] + scratch_shapes,
        compiler_params=_cparams(("arbitrary", "arbitrary")),
        name=name,
    )(*args, w_out, x2, mod4, ffn_nw, wr_t, br_col)


def _even_mixers(proj, hf, lb_logits, hg_nw, ret_nw, bsz, seq, layer, outproj):
    ts = min(MIX_SEQ_TILE, seq)
    rc = min(RET_CHUNK, ts)
    per_b = seq // ts
    hd, w = HEAD_DIM, N_HEADS * HEAD_DIM
    rows = lambda width: pl.BlockSpec((ts, width), lambda b, s: (b * per_b + s, 0))
    full = lambda a: pl.BlockSpec(a.shape, lambda b, s: (0,) * a.ndim)
    inv = ROPE_BASE ** (-jnp.arange(0, hd, 2, dtype=F32) / hd)
    ang = jnp.arange(seq, dtype=F32)[:, None] * inv[None, :]
    cos2 = jnp.concatenate([jnp.cos(ang), jnp.cos(ang)], axis=-1)
    sin2 = jnp.concatenate([-jnp.sin(ang), jnp.sin(ang)], axis=-1)
    tab = pl.BlockSpec((ts, hd), lambda b, s: (s, 0))
    hg_nw, ret_nw = hg_nw.reshape(1, -1), ret_nw.reshape(1, -1)
    scratch = [pltpu.VMEM((2 * N_HEADS, hd, hd), F32), pltpu.VMEM((rc, w), F32), pltpu.VMEM((rc, w), F32),
               pltpu.VMEM((N_HEADS, rc, rc), F32)]
    return _mixer_call(
        functools.partial(_even_mixer_kernel, layer=layer, n_chunks=ts // CHUNK, ret_chunk=rc), bsz, seq, ts,
        [rows(proj.shape[1]), rows(hf.shape[1]), full(lb_logits), full(hg_nw), full(ret_nw), tab, tab],
        scratch, f"even_mixer_l{layer}",
        (proj, hf, lb_logits, hg_nw, ret_nw, cos2, sin2), outproj)


def _odd_mixer(proj, log_a, gla_nw, bsz, seq, layer, outproj):
    ts = min(MIX_SEQ_TILE, seq)
    per_b = seq // ts
    rows = lambda width: pl.BlockSpec((ts, width), lambda b, s: (b * per_b + s, 0))
    gla_nw = gla_nw.reshape(1, -1)
    return _mixer_call(
        functools.partial(_gla_mixer_kernel, n_chunks=ts // CHUNK), bsz, seq, ts,
        [rows(proj.shape[1]), rows(log_a.shape[1]), pl.BlockSpec(gla_nw.shape, lambda b, s: (0, 0))],
        [pltpu.VMEM((N_HEADS, 2 * HEAD_DIM, HEAD_DIM), F32)], f"gla_mixer_l{layer}",
        (proj, log_a, gla_nw), outproj)


def _row_pick(n_rows, rows):
    rid = lax.broadcasted_iota(jnp.int32, (n_rows, rows[0].shape[1]), 0)
    out = jnp.zeros(rid.shape, F32)
    for j, row in reversed(list(enumerate(rows))):
        out = jnp.where(rid == j, row, out)
    return out


def _route(h2, wr_t, br):
    neg = -jnp.inf
    logits = _dot_nt(wr_t, h2)
    tm = logits.shape[1]
    eid = lax.broadcasted_iota(jnp.int32, logits.shape, 0)
    ex = jnp.exp(logits - jnp.max(logits, axis=0, keepdims=True))
    probs = ex / jnp.sum(ex, axis=0, keepdims=True)
    sel = probs + br
    grp = eid // EXPERTS_PER_GROUP
    gs = [jnp.max(jnp.where(grp == g, sel, neg), axis=0, keepdims=True) for g in range(N_GROUPS)]
    best = functools.reduce(jnp.maximum, gs)
    gidx = jnp.full(best.shape, N_GROUPS - 1, jnp.int32)
    for g in range(N_GROUPS - 2, -1, -1):
        gidx = jnp.where(gs[g] == best, g, gidx)
    masked = jnp.where(grp == gidx, sel, neg)
    m1 = jnp.max(masked, axis=0, keepdims=True)
    i1 = jnp.min(jnp.where(masked == m1, eid, N_EXPERTS), axis=0, keepdims=True)
    masked2 = jnp.where(eid == i1, neg, masked)
    m2 = jnp.max(masked2, axis=0, keepdims=True)
    i2 = jnp.min(jnp.where(masked2 == m2, eid, N_EXPERTS), axis=0, keepdims=True)
    p1 = jnp.sum(jnp.where(eid == i1, probs, 0.0), axis=0, keepdims=True)
    p2 = jnp.sum(jnp.where(eid == i2, probs, 0.0), axis=0, keepdims=True)
    den = p1 + p2
    chosen = jnp.where(eid == i1, 1.0, jnp.where(eid == i2, 1.0, 0.0))
    r = lax.broadcasted_iota(jnp.int32, (tm, tm), 0)
    c = lax.broadcasted_iota(jnp.int32, (tm, tm), 1)
    earlier = jnp.where(r < c, 1.0, 0.0).astype(BF16)
    rank = jnp.dot(chosen.astype(BF16), earlier, preferred_element_type=F32)
    rank1 = jnp.sum(jnp.where(eid == i1, rank, 0.0), axis=0, keepdims=True)
    rank2 = jnp.sum(jnp.where(eid == i2, rank, 0.0), axis=0, keepdims=True)
    meta = _row_pick(SUBLANES, [i1.astype(F32), i2.astype(F32), p1 / den, p2 / den, rank1, rank2])
    return meta, jnp.sum(chosen, axis=1, keepdims=True)


def _outproj_kernel(o_ref, w_ref, x_ref, mod_ref, nw_ref, wr_ref, br_ref, x1_ref, h2_ref, meta_ref, cnt_ref):
    y = jnp.dot(o_ref[...], w_ref[...], preferred_element_type=F32)
    x1 = x_ref[...] + mod_ref[2:3, :] * y
    x1_ref[...] = x1
    h2 = _norm_mod(x1, nw_ref[...], mod_ref, 3, 4).astype(BF16)
    h2_ref[...] = h2
    meta, counts = _route(h2, wr_ref[...], br_ref[...])
    meta_ref[...] = meta
    cnt_ref[...] = jnp.broadcast_to(counts, cnt_ref.shape)


class _MoePlan:
    def __init__(self, t, tb):
        self.tb = tb
        self.n_tiles = t // tb
        self.local_rows = -(-(TOP_K * tb + N_EXPERTS * (ROW_GROUP - 1)) // PERM_BLOCK) * PERM_BLOCK
        self.ffn_tile = min(FFN_ROW_TILE, TOP_K * tb)
        worst_rows = TOP_K * t + self.n_tiles * N_EXPERTS * (ROW_GROUP - 1)
        self.ffn_tiles = -(-worst_rows // self.ffn_tile) + N_EXPERTS
        self.sorted_rows = self.ffn_tiles * self.ffn_tile


def _moe_tables(counts, plan):
    i32 = jnp.int32
    n_pad = (counts + (ROW_GROUP - 1)) // ROW_GROUP * ROW_GROUP
    ends = jnp.cumsum(n_pad, axis=1)
    loc = ends - n_pad
    n_e = jnp.sum(n_pad, axis=0)
    n_e_pad = -(-n_e // plan.ffn_tile) * plan.ffn_tile
    region_end = jnp.cumsum(n_e_pad)
    base = region_end - n_e_pad
    goff = base[None, :] + jnp.cumsum(n_pad, axis=0) - n_pad
    n_used = region_end[-1] // plan.ffn_tile
    tile_row = jnp.minimum(jnp.arange(plan.ffn_tiles, dtype=i32), n_used - 1) * plan.ffn_tile
    tile_expert = jnp.minimum(jnp.sum(region_end[None, :] <= tile_row[:, None], axis=-1), N_EXPERTS - 1)
    flat = lambda a: a.reshape(-1).astype(i32)
    return dict(loc=loc, seg_src=flat(loc), seg_dst=flat(goff), seg_rows=flat(n_pad),
                tail_start=(base + n_e).astype(i32), tail_rows=(n_e_pad - n_e).astype(i32),
                tile_expert=tile_expert.astype(i32), n_used=n_used.reshape(1).astype(i32))


def _segment_copy(buf_ref, slot, src_row, hbm_ref, dst_row, n_rows, sem, to_hbm):
    al = lambda v: pl.multiple_of(v, ROW_GROUP)
    local = buf_ref.at[slot, pl.ds(al(src_row), al(n_rows))]
    remote = hbm_ref.at[pl.ds(al(dst_row), al(n_rows))]
    src, dst = (local, remote) if to_hbm else (remote, local)
    return pltpu.make_async_copy(src, dst, sem.at[slot])


def _tile_segments(tile, slot, buf_ref, hbm_ref, sem, src_ref, dst_ref, rows_ref, to_hbm, start):
    def seg(e, c):
        k = tile * N_EXPERTS + e
        n = rows_ref[k]

        @pl.when(n > 0)
        def _():
            cp = _segment_copy(buf_ref, slot, src_ref[k], hbm_ref, dst_ref[k], n, sem, to_hbm)
            cp.start() if start else cp.wait()
        return c

    lax.fori_loop(0, N_EXPERTS, seg, 0)


def _dispatch_kernel(src_ref, dst_ref, rows_ref, tstart_ref, trows_ref, nu_ref, h_ref, meta_ref, loc_ref,
                     sorted_ref, cmeta_ref, buf_ref, zero_ref, sem, zsem, *, plan):
    i = pl.program_id(0)
    last = pl.num_programs(0) - 1
    slot = i % 2
    tb, rows_l = plan.tb, plan.local_rows

    def tails(start):
        def tail(e, c):
            n = trows_ref[e]

            @pl.when(n > 0)
            def _():
                cp = pltpu.make_async_copy(zero_ref.at[pl.ds(0, pl.multiple_of(n, ROW_GROUP))],
                                           sorted_ref.at[pl.ds(pl.multiple_of(tstart_ref[e], ROW_GROUP),
                                                               pl.multiple_of(n, ROW_GROUP))], zsem.at[0])
                cp.start() if start else cp.wait()
            return c

        lax.fori_loop(0, N_EXPERTS, tail, 0)

    def spare_copy(j):
        row = j * plan.ffn_tile if isinstance(j, int) else pl.multiple_of(j * plan.ffn_tile, plan.ffn_tile)
        return pltpu.make_async_copy(zero_ref, sorted_ref.at[pl.ds(row, plan.ffn_tile)], zsem.at[1])

    @pl.when(i == 0)
    def _():
        zero_ref[...] = jnp.zeros_like(zero_ref)
        tails(True)
        lax.fori_loop(nu_ref[0], plan.ffn_tiles, lambda j, c: (spare_copy(j).start(), c)[1], 0)

    def wait_tile(tile, s):
        _tile_segments(tile, s, buf_ref, sorted_ref, sem, src_ref, dst_ref, rows_ref, True, False)

    @pl.when(i >= 2)
    def _():
        wait_tile(i - 2, slot)

    meta = meta_ref[...]
    eid = lax.broadcasted_iota(jnp.int32, (N_EXPERTS, tb), 0).astype(F32)
    loc_col = loc_ref[...]

    def local_row(e_row, rank_row):
        return jnp.sum(jnp.where(eid == e_row, loc_col, 0.0), axis=0, keepdims=True) + rank_row

    r1_row = local_row(meta[0:1, :], meta[4:5, :])
    r2_row = local_row(meta[1:2, :], meta[5:6, :])
    r1h, r2h = jnp.floor(r1_row * (1.0 / 256.0)), jnp.floor(r2_row * (1.0 / 256.0))
    pieces = _row_pick(LANES, [r1h, r1_row - 256.0 * r1h, r2h, r2_row - 256.0 * r2h,
                               meta[2:3, :], meta[3:4, :]]).astype(BF16)
    eye = jnp.where(lax.broadcasted_iota(jnp.int32, (tb, tb), 0) == lax.broadcasted_iota(jnp.int32, (tb, tb), 1),
                    1.0, 0.0).astype(BF16)
    cmeta_ref[...] = _dot_nt(eye, pieces)
    h = h_ref[...]
    blk = PERM_BLOCK
    for rb in range(rows_l // blk):
        rid = (lax.broadcasted_iota(jnp.int32, (blk, tb), 0) + rb * blk).astype(F32)
        perm = jnp.where(rid == r1_row, 1.0, jnp.where(rid == r2_row, 1.0, 0.0)).astype(BF16)
        buf_ref[slot, rb * blk:(rb + 1) * blk, :] = jnp.dot(perm, h, preferred_element_type=F32).astype(BF16)

    _tile_segments(i, slot, buf_ref, sorted_ref, sem, src_ref, dst_ref, rows_ref, True, True)

    @pl.when(i == last)
    def _():
        @pl.when(i >= 1)
        def _():
            wait_tile(i - 1, 1 - slot)
        wait_tile(i, slot)
        tails(False)
        lax.fori_loop(nu_ref[0], plan.ffn_tiles, lambda j, c: (spare_copy(0).wait(), c)[1], 0)


def _dispatch(h2, meta, tabs, plan):
    t, d = h2.shape
    loc_col = tabs["loc"].astype(F32)[:, :, None]
    grid_spec = pltpu.PrefetchScalarGridSpec(
        num_scalar_prefetch=6,
        grid=(plan.n_tiles,),
        in_specs=[pl.BlockSpec((plan.tb, d), lambda i, *_: (i, 0)),
                  pl.BlockSpec((None, SUBLANES, plan.tb), lambda i, *_: (i, 0, 0)),
                  pl.BlockSpec((None, N_EXPERTS, 1), lambda i, *_: (i, 0, 0))],
        out_specs=[pl.BlockSpec(memory_space=pl.ANY),
                   pl.BlockSpec((plan.tb, LANES), lambda i, *_: (i, 0))],
        scratch_shapes=[pltpu.VMEM((2, plan.local_rows, d), BF16),
                        pltpu.VMEM((plan.ffn_tile, d), BF16),
                        pltpu.SemaphoreType.DMA((2,)),
                        pltpu.SemaphoreType.DMA((2,))])
    return pl.pallas_call(
        functools.partial(_dispatch_kernel, plan=plan),
        grid_spec=grid_spec,
        out_shape=[jax.ShapeDtypeStruct((plan.sorted_rows, d), BF16),
                   jax.ShapeDtypeStruct((t, LANES), F32)],
        compiler_params=_cparams(("arbitrary",)),
        name="moe_dispatch",
    )(tabs["seg_src"], tabs["seg_dst"], tabs["seg_rows"], tabs["tail_start"], tabs["tail_rows"], tabs["n_used"],
      h2, meta, loc_col)


def _ffn_kernel(te_ref, nu_ref, x_ref, wg_ref, wu_ref, wd_ref, y_ref, wg_s, wu_s, wd_s):
    j = pl.program_id(0)
    used = j < nu_ref[0]
    new_expert = jnp.logical_or(j == 0, te_ref[j] != te_ref[jnp.maximum(j - 1, 0)])

    @pl.when(jnp.logical_and(used, new_expert))
    def _():
        wg_s[...] = wg_ref[0].astype(BF16)
        wu_s[...] = wu_ref[0].astype(BF16)
        wd_s[...] = wd_ref[0].astype(BF16)

    @pl.when(used)
    def _():
        x = x_ref[...]
        g = jnp.dot(x, wg_s[...], preferred_element_type=F32)
        u = jnp.dot(x, wu_s[...], preferred_element_type=F32)
        y_ref[...] = jnp.dot((_silu(g) * u).astype(BF16), wd_s[...], preferred_element_type=F32).astype(BF16)

    @pl.when(jnp.logical_not(used))
    def _():
        y_ref[...] = jnp.zeros_like(y_ref)


def _ffn(xs, wg, wu, wd, layer, tabs, plan):
    d, de = wg.shape[2], wg.shape[3]
    tm = plan.ffn_tile
    row_blk = lambda j, te, nu: (jnp.minimum(j, nu[0] - 1), 0)
    grid_spec = pltpu.PrefetchScalarGridSpec(
        num_scalar_prefetch=2,
        grid=(plan.ffn_tiles,),
        in_specs=[pl.BlockSpec((tm, d), row_blk),
                  pl.BlockSpec((None, 1, d, de), lambda j, te, nu: (layer, te[j], 0, 0)),
                  pl.BlockSpec((None, 1, d, de), lambda j, te, nu: (layer, te[j], 0, 0)),
                  pl.BlockSpec((None, 1, de, d), lambda j, te, nu: (layer, te[j], 0, 0))],
        out_specs=pl.BlockSpec((tm, d), lambda j, te, nu: (j, 0)),
        scratch_shapes=[pltpu.VMEM((d, de), BF16), pltpu.VMEM((d, de), BF16), pltpu.VMEM((de, d), BF16)])
    return pl.pallas_call(
        _ffn_kernel,
        grid_spec=grid_spec,
        out_shape=jax.ShapeDtypeStruct(xs.shape, BF16),
        compiler_params=_cparams(("arbitrary",)),
        name="moe_ffn",
    )(tabs["tile_expert"], tabs["n_used"], xs, wg, wu, wd)


def _combine_kernel(src_ref, dst_ref, rows_ref, ys_ref, cm0_ref, cmn_ref, x_ref, mod_ref, fw_ref, *refs,
                    plan, final, next_body, n_w):
    if next_body is not None:
        modn_ref, nwn_ref = refs[:2]
        wn_refs = refs[2:2 + n_w]
        o_ref, *next_out = refs[2 + n_w:5 + n_w]
        buf_ref, comb_a, comb_b, sem = refs[5 + n_w:]
    else:
        o_ref, buf_ref, comb_a, comb_b, sem = refs
    comb_refs = (comb_a, comb_b)
    i = pl.program_id(0)
    n_tiles = pl.num_programs(0)
    slot = i % GATHER_RING
    ahead = GATHER_RING - 1

    def fetch(tile, s, start):
        _tile_segments(tile, s, buf_ref, ys_ref, sem, src_ref, dst_ref, rows_ref, False, start)

    @pl.when(i == 0)
    def _():
        buf_ref[...] = jnp.zeros_like(buf_ref)
        fetch(0, 0, True)
        for k in range(1, ahead):
            pl.when(k < n_tiles)(functools.partial(fetch, k, k, True))

    @pl.when(i + ahead < n_tiles)
    def _():
        fetch(i + ahead, (i + ahead) % GATHER_RING, True)

    fetch(i, slot, False)

    def build(cm_ref, comb_ref):
        cm = cm_ref[...]
        r1 = cm[:, 0:1] * 256.0 + cm[:, 1:2]
        r2 = cm[:, 2:3] * 256.0 + cm[:, 3:4]
        rid = lax.broadcasted_iota(jnp.int32, (plan.tb, plan.local_rows), 1).astype(F32)
        comb_ref[...] = (jnp.where(rid == r1, cm[:, 4:5], 0.0)
                         + jnp.where(rid == r2, cm[:, 5:6], 0.0)).astype(BF16)

    @pl.when(i == 0)
    def _():
        build(cm0_ref, comb_refs[0])

    def step(parity):
        build(cmn_ref, comb_refs[1 - parity])
        moe = jnp.dot(comb_refs[parity][...], buf_ref[slot], preferred_element_type=F32)
        xn = x_ref[...] + mod_ref[5:6, :] * moe
        if final:
            ms = jnp.mean(xn * xn, axis=-1, keepdims=True)
            xn = xn * lax.rsqrt(ms + EPS) * fw_ref[...]
        o_ref[...] = xn
        if next_body is not None:
            next_body(_norm_mod(xn, nwn_ref[...], modn_ref, 0, 1).astype(BF16), wn_refs, next_out)

    for parity in range(2):
        pl.when(i % 2 == parity)(functools.partial(step, parity))


def _combine(ys, cmeta, x1, mod4, layer, seq, tabs, plan, final_w, final, next_proj=None):
    t, d = x1.shape
    per_b = seq // plan.tb
    rows = lambda width: pl.BlockSpec((plan.tb, width), lambda i, *_: (i, 0))
    in_specs = [pl.BlockSpec(memory_space=pl.ANY),
                pl.BlockSpec((plan.tb, LANES), lambda i, *_: (0, 0)),
                pl.BlockSpec((plan.tb, LANES), lambda i, *_: (jnp.minimum(i + 1, plan.n_tiles - 1), 0)),
                rows(d),
                pl.BlockSpec((None, None, 6, d), lambda i, *_: (layer, i // per_b, 0, 0)),
                pl.BlockSpec((1, d), lambda i, *_: (0, 0))]
    args = [ys, cmeta, cmeta, x1, mod4, final_w.reshape(1, d)]
    out_specs, out_shape = [rows(d)], [jax.ShapeDtypeStruct((t, d), F32)]
    body, n_w = None, 0
    if next_proj is not None:
        nw_next, weights, body = next_proj
        n_w = len(weights)
        in_specs += ([pl.BlockSpec((None, None, 6, d), lambda i, *_: (layer + 1, i // per_b, 0, 0)),
                      pl.BlockSpec((1, d), lambda i, *_: (0, 0))]
                     + [pl.BlockSpec(w.shape, lambda i, *_: (0, 0)) for w in weights])
        args += [mod4, nw_next.reshape(1, d)] + list(weights)
        proj_specs, proj_shape = _inproj_out(t, plan.tb, weights[0].shape[1], lambda i, *_: (i, 0))
        out_specs, out_shape = out_specs + proj_specs, out_shape + proj_shape
    grid_spec = pltpu.PrefetchScalarGridSpec(
        num_scalar_prefetch=3,
        grid=(plan.n_tiles,),
        in_specs=in_specs,
        out_specs=out_specs,
        scratch_shapes=[pltpu.VMEM((GATHER_RING, plan.local_rows, d), BF16),
                        pltpu.VMEM((plan.tb, plan.local_rows), BF16),
                        pltpu.VMEM((plan.tb, plan.local_rows), BF16),
                        pltpu.SemaphoreType.DMA((GATHER_RING,))])
    return pl.pallas_call(
        functools.partial(_combine_kernel, plan=plan, final=final, next_body=body, n_w=n_w),
        grid_spec=grid_spec,
        out_shape=out_shape,
        compiler_params=_cparams(("arbitrary",)),
        name="moe_combine",
    )(tabs["seg_src"], tabs["seg_dst"], tabs["seg_rows"], *args)


def kernel(x, c, w_ada, b_ada, norm_mix_w, norm_ffn_w, w_in_even, hg_lb_logits, hg_norm_w, ret_norm_w,
           w_out_even, w_in_odd, w_gla_a2, b_gla_a2, gla_norm_w, w_out_odd, w_router, b_router,
           w_gate, w_up, w_down, final_norm_w):
    bsz, seq, d = x.shape
    depth = w_ada.shape[0]
    x2 = x.reshape(bsz * seq, d)
    mod4 = _adaln_mod(c, w_ada, b_ada).reshape(depth, bsz, 6, d)
    wr_t = w_router.T.astype(BF16)
    br_col = b_router.reshape(N_EXPERTS, 1)
    plan = _MoePlan(bsz * seq, min(MOE_TOKEN_TILE, seq))

    def inproj_args(l):
        j = l // 2
        if l % 2 == 0:
            return norm_mix_w[l], [w_in_even[j].astype(BF16)], _inproj_even_body
        n_main = w_in_odd.shape[-1] - GLA_RANK
        wi = w_in_odd[j]
        wa = jnp.pad(wi[:, n_main:], ((0, 0), (0, LANES - GLA_RANK))).astype(BF16)
        wa2 = jnp.pad(w_gla_a2[j], ((0, LANES - GLA_RANK), (0, 0))).astype(BF16)
        return norm_mix_w[l], [wi[:, :n_main].astype(BF16), wa, wa2, b_gla_a2[j].reshape(1, -1)], _inproj_odd_body

    fuse_next = plan.tb == min(INPROJ_ROW_TILE, seq)
    projected = None
    for l in range(depth):
        j = l // 2
        proj, aux = projected if projected is not None else _inproj(x2, mod4, l, seq, *inproj_args(l))
        w_out = (w_out_even if l % 2 == 0 else w_out_odd)[j].astype(BF16)
        outproj = (w_out, x2, mod4, l, norm_ffn_w[l], wr_t, br_col)
        if l % 2 == 0:
            routed = _even_mixers(proj, aux, hg_lb_logits, hg_norm_w[j], ret_norm_w[j], bsz, seq, l, outproj)
        else:
            routed = _odd_mixer(proj, aux, gla_norm_w[j], bsz, seq, l, outproj)
        x1, h2, meta, counts = routed
        tabs = _moe_tables(counts[:, :, 0].astype(jnp.int32), plan)
        xs, cmeta = _dispatch(h2, meta, tabs, plan)
        ys = _ffn(xs, w_gate, w_up, w_down, l, tabs, plan)
        last = l == depth - 1
        outs = _combine(ys, cmeta, x1, mod4, l, seq, tabs, plan, final_norm_w, final=last,
                        next_proj=inproj_args(l + 1) if fuse_next and not last else None)
        x2, projected = outs[0], (tuple(outs[1:]) if len(outs) > 1 else None)
    return x2.reshape(bsz, seq, d)
```
